```python
import math
import jax, jax.numpy as jnp
from jax import lax
import numpy as np

D_MODEL = 2048
BATCH = 16
SEQ = 2048
DEPTH = 2

N_META = 16
CHUNK = 128
NORM_EPS = 1e-6
HEAD_NORM_EPS = 1e-5
S5_WIDTH = D_MODEL // 2
S5_GROUP_SIZE = 16
S5_GROUPS = S5_WIDTH // S5_GROUP_SIZE
S5_STATE = 64
MLSTM_WIDTH = 3 * D_MODEL // 2
MLSTM_HEADS = 8
MLSTM_HEAD_DIM = MLSTM_WIDTH // MLSTM_HEADS
MLSTM_CONV = 4
QKV_BLOCK = 4
AB_INNER = S5_WIDTH + MLSTM_WIDTH
AB_IN = 2 * AB_INNER
SSD_INNER = 2 * D_MODEL
SSD_HEAD_DIM = 64
SSD_HEADS = SSD_INNER // SSD_HEAD_DIM
SSD_STATE = 128
SSD_GROUPS = 8
SSD_HPG = SSD_HEADS // SSD_GROUPS
SSD_CONV = 4
SSD_CONV_DIM = SSD_INNER + 2 * SSD_GROUPS * SSD_STATE
SSD_IN = SSD_INNER + SSD_CONV_DIM + SSD_HEADS
N_EVEN = (DEPTH + 1) // 2
N_ODD = DEPTH // 2

kernel_name = 'hybrid_s5_mlstm_ssd_meta'

F32 = jnp.float32


def rmsnorm(x, g):
    xf = x.astype(F32)
    y = xf * lax.rsqrt(jnp.mean(xf * xf, axis=-1, keepdims=True) + NORM_EPS)
    return (y * g.astype(F32)).astype(x.dtype)


def causal_dwconv(x, w, b):
    k, c = w.shape
    y = lax.conv_general_dilated(x, w[:, None, :].astype(x.dtype), window_strides=(1,),
                                 padding=[(k - 1, 0)], dimension_numbers=('NWC', 'WIO', 'NWC'),
                                 feature_group_count=c)
    return y + b.astype(x.dtype)


def run_chunked(step, carry, seqs):
    bsz = seqs[0].shape[0]
    carry, y_meta = step(carry, tuple(a[:, :N_META] for a in seqs))

    def to_chunks(a):
        r = a[:, N_META:]
        r = r.reshape(bsz, r.shape[1] // CHUNK, CHUNK, *r.shape[2:])
        return jnp.moveaxis(r, 1, 0)

    _, y_real = lax.scan(step, carry, tuple(to_chunks(a) for a in seqs))
    y_real = jnp.moveaxis(y_real, 0, 1)
    y_real = y_real.reshape(bsz, -1, *y_real.shape[3:])
    return jnp.concatenate([y_meta, y_real], axis=1)


def cmul(ar, ai, br, bi):
    return ar * br - ai * bi, ar * bi + ai * br


def s5_mixer(u, lam_re, lam_im, log_dt, b_re, b_im, c_re, c_im, d, glu_w, glu_b):
    bsz, t_len, _ = u.shape
    uf = u.astype(F32).reshape(bsz, t_len, S5_GROUPS, S5_GROUP_SIZE)
    dt = jnp.exp(log_dt.astype(F32))[:, None]
    lr, li = lam_re.astype(F32), lam_im.astype(F32)
    mag = jnp.exp(lr * dt)
    ar, ai = mag * jnp.cos(li * dt), mag * jnp.sin(li * dt)
    den = lr * lr + li * li
    qr = ((ar - 1.0) * lr + ai * li) / den
    qi = (ai * lr - (ar - 1.0) * li) / den
    bbr, bbi = cmul(qr[..., None], qi[..., None], b_re.astype(F32), b_im.astype(F32))
    cr, ci = c_re.astype(F32), c_im.astype(F32)

    def combine(e1, e2):
        a1r, a1i, s1r, s1i = e1
        a2r, a2i, s2r, s2i = e2
        pr, pi_ = cmul(a1r, a1i, a2r, a2i)
        tr, ti = cmul(a2r, a2i, s1r, s1i)
        return pr, pi_, tr + s2r, ti + s2i

    def step(carry, inp):
        sr, si = carry
        (uc,) = inp
        bur = jnp.einsum('bcgh,gph->bcgp', uc, bbr)
        bui = jnp.einsum('bcgh,gph->bcgp', uc, bbi)
        shp = bur.shape
        pr, pi_, xr, xi = lax.associative_scan(
            combine, (jnp.broadcast_to(ar, shp), jnp.broadcast_to(ai, shp), bur, bui), axis=1)
        hr, hi = cmul(pr, pi_, sr[:, None], si[:, None])
        xr, xi = xr + hr, xi + hi
        y = jnp.einsum('ghp,bcgp->bcgh', cr, xr) - jnp.einsum('ghp,bcgp->bcgh', ci, xi)
        return (xr[:, -1], xi[:, -1]), y

    zeros = jnp.zeros((bsz, S5_GROUPS, S5_STATE), F32)
    y = run_chunked(step, (zeros, zeros), (uf,))
    y = (y + d.astype(F32).reshape(S5_GROUPS, S5_GROUP_SIZE) * uf).reshape(bsz, t_len, S5_WIDTH)
    g = jax.nn.gelu(y)
    return g * jax.nn.sigmoid(g @ glu_w.astype(F32) + glu_b.astype(F32))


def mlstm_mixer(xm, conv_w, conv_b, wq, wk, wv, w_gate, b_gate, norm_w, skip):
    bsz, t_len, _ = xm.shape
    xf = xm.astype(F32)
    xc = jax.nn.silu(causal_dwconv(xf, conv_w.astype(F32), conv_b.astype(F32)))

    def headwise(a, w):
        nb = w.shape[0]
        return jnp.einsum('btni,nio->btno', a.reshape(bsz, t_len, nb, QKV_BLOCK),
                          w.astype(F32)).reshape(bsz, t_len, MLSTM_WIDTH)

    q = headwise(xc, wq)
    k = headwise(xc, wk)
    v = headwise(xf, wv)
    wg = w_gate.astype(F32)
    gates = (q @ wg[:MLSTM_WIDTH] + k @ wg[MLSTM_WIDTH:2 * MLSTM_WIDTH]
             + v @ wg[2 * MLSTM_WIDTH:] + b_gate.astype(F32))
    ig = gates[..., :MLSTM_HEADS]
    lf = jax.nn.log_sigmoid(gates[..., MLSTM_HEADS:])
    hs = (bsz, t_len, MLSTM_HEADS, MLSTM_HEAD_DIM)
    qh = q.reshape(hs) * (MLSTM_HEAD_DIM ** -0.5)
    kh = k.reshape(hs)
    vh = v.reshape(hs)

    def step(carry, inp):
        cmat, nvec, m = carry
        qc, kc, vc, igc, lfc = inp
        c = qc.shape[1]
        bcum = jnp.cumsum(lfc, axis=1)
        causal = jnp.tril(jnp.ones((c, c), bool))[None, :, :, None]
        dmat = bcum[:, :, None, :] - bcum[:, None, :, :] + igc[:, None, :, :]
        dmat = jnp.where(causal, dmat, -jnp.inf)
        inter = bcum + m[:, None, :]
        mt = jnp.maximum(inter, dmat.max(axis=2))
        wt = jnp.exp(dmat - mt[:, :, None, :])
        w_prev = jnp.exp(inter - mt)
        s = jnp.einsum('bthd,bshd->btsh', qc, kc) * wt
        num = (jnp.einsum('btsh,bshe->bthe', s, vc)
               + w_prev[..., None] * jnp.einsum('bthd,bhde->bthe', qc, cmat))
        den = s.sum(axis=2) + w_prev * jnp.einsum('bthd,bhd->bth', qc, nvec)
        h = num / jnp.maximum(jnp.abs(den), jnp.exp(-mt))[..., None]
        blast = bcum[:, -1]
        g = blast[:, None, :] - bcum + igc
        m_new = jnp.maximum(blast + m, g.max(axis=1))
        decay = jnp.exp(blast + m - m_new)
        wkc = jnp.exp(g - m_new[:, None, :])[..., None] * kc
        c_new = decay[..., None, None] * cmat + jnp.einsum('bshd,bshe->bhde', wkc, vc)
        n_new = decay[..., None] * nvec + wkc.sum(axis=1)
        return (c_new, n_new, m_new), h

    carry0 = (jnp.zeros((bsz, MLSTM_HEADS, MLSTM_HEAD_DIM, MLSTM_HEAD_DIM), F32),
              jnp.zeros((bsz, MLSTM_HEADS, MLSTM_HEAD_DIM), F32),
              jnp.zeros((bsz, MLSTM_HEADS), F32))
    h = run_chunked(step, carry0, (qh, kh, vh, ig, lf))
    mu = jnp.mean(h, axis=-1, keepdims=True)
    var = jnp.mean(jnp.square(h - mu), axis=-1, keepdims=True)
    hn = (h - mu) * lax.rsqrt(var + HEAD_NORM_EPS) * norm_w.astype(F32).reshape(MLSTM_HEADS, MLSTM_HEAD_DIM)
    return hn.reshape(bsz, t_len, MLSTM_WIDTH) + skip.astype(F32) * xc


def ab_layer(x, norm_g, w_in, s5_lambda_re, s5_lambda_im, s5_log_dt, s5_b_re, s5_b_im, s5_c_re,
             s5_c_im, s5_d, s5_glu_w, s5_glu_b, ml_conv_w, ml_conv_b, ml_wq, ml_wk, ml_wv,
             ml_w_gate, ml_b_gate, ml_norm, ml_skip, w_out):
    p = rmsnorm(x, norm_g) @ w_in
    u_a, z_a, x_b, z_b = jnp.split(p, [S5_WIDTH, 2 * S5_WIDTH, 2 * S5_WIDTH + MLSTM_WIDTH], axis=-1)
    y_a = s5_mixer(u_a, s5_lambda_re, s5_lambda_im, s5_log_dt, s5_b_re, s5_b_im, s5_c_re, s5_c_im,
                   s5_d, s5_glu_w, s5_glu_b) * jax.nn.silu(z_a.astype(F32))
    y_b = mlstm_mixer(x_b, ml_conv_w, ml_conv_b, ml_wq, ml_wk, ml_wv, ml_w_gate, ml_b_gate,
                      ml_norm, ml_skip) * jax.nn.silu(z_b.astype(F32))
    y = jnp.concatenate([y_a, y_b], axis=-1).astype(x.dtype)
    return x + y @ w_out


def ssd_layer(x, norm_g, w_in, conv_w, conv_b, dt_bias, a_log, d, gnorm, w_out):
    bsz, t_len, _ = x.shape
    p = (rmsnorm(x, norm_g) @ w_in).astype(F32)
    z, xbc, dt = jnp.split(p, [SSD_INNER, SSD_INNER + SSD_CONV_DIM], axis=-1)
    xbc = jax.nn.silu(causal_dwconv(xbc, conv_w.astype(F32), conv_b.astype(F32)))
    xs, bm, cm = jnp.split(xbc, [SSD_INNER, SSD_INNER + SSD_GROUPS * SSD_STATE], axis=-1)
    xs = xs.reshape(bsz, t_len, SSD_GROUPS, SSD_HPG, SSD_HEAD_DIM)
    bm = bm.reshape(bsz, t_len, SSD_GROUPS, SSD_STATE)
    cm = cm.reshape(bsz, t_len, SSD_GROUPS, SSD_STATE)
    dt = jax.nn.softplus(dt + dt_bias.astype(F32)).reshape(bsz, t_len, SSD_GROUPS, SSD_HPG)
    a = -jnp.exp(a_log.astype(F32)).reshape(SSD_GROUPS, SSD_HPG)

    def step(state, inp):
        xc, dtc, bc, cc = inp
        c = xc.shape[1]
        cum = jnp.cumsum(dtc * a, axis=1)
        causal = jnp.tril(jnp.ones((c, c), bool))[None, :, :, None, None]
        seg = jnp.exp(jnp.where(causal, cum[:, :, None] - cum[:, None], -jnp.inf))
        cb = jnp.einsum('btgn,bsgn->btsg', cc, bc)
        w = cb[..., None] * seg * dtc[:, None]
        y = (jnp.einsum('btsgr,bsgrp->btgrp', w, xc)
             + jnp.exp(cum)[..., None] * jnp.einsum('btgn,bgrpn->btgrp', cc, state))
        last = cum[:, -1]
        dec = jnp.exp(last[:, None] - cum) * dtc
        state = (jnp.exp(last)[..., None, None] * state
                 + jnp.einsum('bsgr,bsgn,bsgrp->bgrpn', dec, bc, xc))
        return state, y

    state0 = jnp.zeros((bsz, SSD_GROUPS, SSD_HPG, SSD_HEAD_DIM, SSD_STATE), F32)
    y = run_chunked(step, state0, (xs, dt, bm, cm))
    y = y + d.astype(F32).reshape(SSD_GROUPS, SSD_HPG, 1) * xs
    y = y.reshape(bsz, t_len, SSD_INNER) * jax.nn.silu(z)
    yg = y.reshape(bsz, t_len, SSD_GROUPS, -1)
    yg = yg * lax.rsqrt(jnp.mean(yg * yg, axis=-1, keepdims=True) + NORM_EPS)
    y = yg.reshape(bsz, t_len, SSD_INNER) * gnorm.astype(F32)
    return x + y.astype(x.dtype) @ w_out


def setup_inputs(seed: int = 0) -> dict:
    key = jax.random.key(seed)
    ks = iter(jax.random.split(key, 48))
    nrm = lambda shape, s: jax.random.normal(next(ks), shape, F32) * s
    ne, no = N_EVEN, N_ODD
    lam_im = jnp.pi * jnp.arange(S5_STATE, dtype=F32)
    gate_b = jnp.concatenate([
        nrm((ne, MLSTM_HEADS), 0.1),
        jnp.linspace(3.0, 6.0, MLSTM_HEADS, dtype=F32)[None] + nrm((ne, MLSTM_HEADS), 0.01)], axis=-1)
    dt0 = jnp.exp(jax.random.uniform(next(ks), (no, SSD_HEADS), F32, math.log(1e-3), math.log(1e-1)))
    return {
        'x': nrm((BATCH, SEQ, D_MODEL), 1.0),
        'meta_tokens': nrm((N_META, D_MODEL), 1.0),
        'ab_norm': 1.0 + nrm((ne, D_MODEL), 0.02),
        'ab_w_in': nrm((ne, D_MODEL, AB_IN), D_MODEL ** -0.5),
        's5_lambda_re': -0.5 + nrm((ne, S5_GROUPS, S5_STATE), 0.01),
        's5_lambda_im': lam_im + nrm((ne, S5_GROUPS, S5_STATE), 0.01),
        's5_log_dt': jax.random.uniform(next(ks), (ne, S5_GROUPS), F32, math.log(1e-3), math.log(1e-1)),
        's5_b_re': nrm((ne, S5_GROUPS, S5_STATE, S5_GROUP_SIZE), (2 * S5_GROUP_SIZE) ** -0.5),
        's5_b_im': nrm((ne, S5_GROUPS, S5_STATE, S5_GROUP_SIZE), (2 * S5_GROUP_SIZE) ** -0.5),
        's5_c_re': nrm((ne, S5_GROUPS, S5_GROUP_SIZE, S5_STATE), (2 * S5_STATE) ** -0.5),
        's5_c_im': nrm((ne, S5_GROUPS, S5_GROUP_SIZE, S5_STATE), (2 * S5_STATE) ** -0.5),
        's5_d': nrm((ne, S5_WIDTH), 1.0),
        's5_glu_w': nrm((ne, S5_WIDTH, S5_WIDTH), S5_WIDTH ** -0.5),
        's5_glu_b': nrm((ne, S5_WIDTH), 0.01),
        'ml_conv_w': nrm((ne, MLSTM_CONV, MLSTM_WIDTH), MLSTM_CONV ** -0.5),
        'ml_conv_b': nrm((ne, MLSTM_WIDTH), 0.01),
        'ml_wq': nrm((ne, MLSTM_WIDTH // QKV_BLOCK, QKV_BLOCK, QKV_BLOCK), QKV_BLOCK ** -0.5),
        'ml_wk': nrm((ne, MLSTM_WIDTH // QKV_BLOCK, QKV_BLOCK, QKV_BLOCK), QKV_BLOCK ** -0.5),
        'ml_wv': nrm((ne, MLSTM_WIDTH // QKV_BLOCK, QKV_BLOCK, QKV_BLOCK), QKV_BLOCK ** -0.5),
        'ml_w_gate': nrm((ne, 3 * MLSTM_WIDTH, 2 * MLSTM_HEADS), (3 * MLSTM_WIDTH) ** -0.5),
        'ml_b_gate': gate_b,
        'ml_norm': 1.0 + nrm((ne, MLSTM_WIDTH), 0.02),
        'ml_skip': 1.0 + nrm((ne, MLSTM_WIDTH), 0.02),
        'ab_w_out': nrm((ne, AB_INNER, D_MODEL), AB_INNER ** -0.5),
        'ssd_norm': 1.0 + nrm((no, D_MODEL), 0.02),
        'ssd_w_in': nrm((no, D_MODEL, SSD_IN), D_MODEL ** -0.5),
        'ssd_conv_w': nrm((no, SSD_CONV, SSD_CONV_DIM), SSD_CONV ** -0.5),
        'ssd_conv_b': nrm((no, SSD_CONV_DIM), 0.01),
        'ssd_dt_bias': dt0 + jnp.log(-jnp.expm1(-dt0)),
        'ssd_a_log': jnp.log(jax.random.uniform(next(ks), (no, SSD_HEADS), F32, 1.0, 16.0)),
        'ssd_d': 1.0 + nrm((no, SSD_HEADS), 0.02),
        'ssd_gnorm': 1.0 + nrm((no, SSD_INNER), 0.02),
        'ssd_w_out': nrm((no, SSD_INNER, D_MODEL), SSD_INNER ** -0.5),
        'final_norm': 1.0 + nrm((D_MODEL,), 0.02),
    }


def reference(x, meta_tokens, ab_norm, ab_w_in, s5_lambda_re, s5_lambda_im, s5_log_dt, s5_b_re,
              s5_b_im, s5_c_re, s5_c_im, s5_d, s5_glu_w, s5_glu_b, ml_conv_w, ml_conv_b, ml_wq,
              ml_wk, ml_wv, ml_w_gate, ml_b_gate, ml_norm, ml_skip, ab_w_out, ssd_norm, ssd_w_in,
              ssd_conv_w, ssd_conv_b, ssd_dt_bias, ssd_a_log, ssd_d, ssd_gnorm, ssd_w_out, final_norm):
    bsz = x.shape[0]
    meta = jnp.broadcast_to(meta_tokens[None].astype(x.dtype), (bsz, N_META, x.shape[-1]))
    h = jnp.concatenate([meta, x], axis=1)
    for layer in range(DEPTH):
        i = layer // 2
        if layer % 2 == 0:
            h = ab_layer(h, ab_norm[i], ab_w_in[i], s5_lambda_re[i], s5_lambda_im[i], s5_log_dt[i],
                         s5_b_re[i], s5_b_im[i], s5_c_re[i], s5_c_im[i], s5_d[i], s5_glu_w[i],
                         s5_glu_b[i], ml_conv_w[i], ml_conv_b[i], ml_wq[i], ml_wk[i], ml_wv[i],
                         ml_w_gate[i], ml_b_gate[i], ml_norm[i], ml_skip[i], ab_w_out[i])
        else:
            h = ssd_layer(h, ssd_norm[i], ssd_w_in[i], ssd_conv_w[i], ssd_conv_b[i], ssd_dt_bias[i],
                          ssd_a_log[i], ssd_d[i], ssd_gnorm[i], ssd_w_out[i])
    return rmsnorm(h, final_norm)[:, N_META:]
```

```python
import functools

import jax
import jax.numpy as jnp
from jax import lax
from jax.experimental import pallas as pl
from jax.experimental.pallas import tpu as pltpu

F32 = jnp.float32
BF16 = jnp.bfloat16

N_META = 16
CHUNK = 128
NORM_EPS = 1e-6
HEAD_NORM_EPS = 1e-5
LANES = 128
SUBLANES = 8
VMEM_LIMIT = 56 * 1024 * 1024

S5_GROUP_SIZE = 16
S5_STATE = 64
S5_OCT = 8
MLSTM_HEADS = 8
QKV_BLOCK = 4
QKV_TILE = 256
SSD_HEAD_DIM = 64
SSD_STATE = 128
SSD_GROUPS = 8


def _params(n_grid):
    return pltpu.CompilerParams(dimension_semantics=("arbitrary",) * n_grid,
                                vmem_limit_bytes=VMEM_LIMIT)


def _const_spec(shape):
    nd = len(shape)
    return pl.BlockSpec(shape, lambda *_: (0,) * nd)


def _dot(a, b):
    return jnp.dot(a.astype(BF16), b.astype(BF16), preferred_element_type=F32)


def _dot_nt(a, b):
    return lax.dot_general(a.astype(BF16), b.astype(BF16), (((1,), (1,)), ((), ())),
                           preferred_element_type=F32)


def _dot_tn(a, b):
    return lax.dot_general(a.astype(BF16), b.astype(BF16), (((0,), (0,)), ((), ())),
                           preferred_element_type=F32)


def _dot_f32(a, b):
    return jnp.dot(a, b, preferred_element_type=F32, precision=lax.Precision.HIGHEST)


def _silu(x):
    return x * (1.0 / (1.0 + jnp.exp(-x)))


def _sigmoid(x):
    return 1.0 / (1.0 + jnp.exp(-x))


def _shift_rows(x, d, prev8):
    rolled = pltpu.roll(x, d, axis=0)
    row8 = lax.broadcasted_iota(jnp.int32, (SUBLANES, x.shape[1]), 0)
    head = jnp.where(row8 < d, pltpu.roll(prev8, d, axis=0), rolled[:SUBLANES])
    return jnp.concatenate([head, rolled[SUBLANES:]], axis=0)


def _causal_conv_silu(x, tail8, w_ref, b_ref):
    k = w_ref.shape[0]
    acc = x * w_ref[k - 1:k, :] + b_ref[...]
    for d in range(1, k):
        acc = acc + _shift_rows(x, d, tail8) * w_ref[k - 1 - d:k - d, :]
    return _silu(acc)


def _rmsnorm_kernel(x_ref, g_ref, o_ref):
    x = x_ref[...]
    y = x * lax.rsqrt(jnp.mean(x * x, axis=-1, keepdims=True) + NORM_EPS)
    o_ref[...] = (y * g_ref[...]).astype(o_ref.dtype)


def _rmsnorm(x, g, out_dtype):
    m, d = x.shape
    tm = min(m, 512)
    return pl.pallas_call(
        _rmsnorm_kernel,
        grid=(m // tm,),
        in_specs=[pl.BlockSpec((tm, d), lambda i: (i, 0)), _const_spec((1, d))],
        out_specs=pl.BlockSpec((tm, d), lambda i: (i, 0)),
        out_shape=jax.ShapeDtypeStruct((m, d), out_dtype),
        compiler_params=_params(1),
        name="rmsnorm",
    )(x, g.reshape(1, d))


def _matmul_kernel(*refs, n_a, has_res):
    a_refs, w_refs = refs[:n_a], refs[n_a:2 * n_a]
    o_ref = refs[-1]
    acc = jnp.dot(a_refs[0][...], w_refs[0][...], preferred_element_type=F32)
    for a_ref, w_ref in zip(a_refs[1:], w_refs[1:]):
        acc = acc + jnp.dot(a_ref[...], w_ref[...], preferred_element_type=F32)
    if has_res:
        acc = acc + refs[2 * n_a][...]
    o_ref[...] = acc.astype(o_ref.dtype)


def _matmul(a_list, w_list, res=None, name="matmul"):
    m = a_list[0].shape[0]
    n = w_list[0].shape[1]
    tm = min(m, 1024)
    tn = 512 if n % 512 == 0 else n
    in_specs = [pl.BlockSpec((tm, a.shape[1]), lambda i, j: (i, 0)) for a in a_list]
    in_specs += [pl.BlockSpec((w.shape[0], tn), lambda i, j: (0, j)) for w in w_list]
    args = list(a_list) + list(w_list)
    if res is not None:
        in_specs.append(pl.BlockSpec((tm, tn), lambda i, j: (i, j)))
        args.append(res)
    return pl.pallas_call(
        functools.partial(_matmul_kernel, n_a=len(a_list), has_res=res is not None),
        grid=(m // tm, n // tn),
        in_specs=in_specs,
        out_specs=pl.BlockSpec((tm, tn), lambda i, j: (i, j)),
        out_shape=jax.ShapeDtypeStruct((m, n), F32),
        compiler_params=_params(2),
        name=name,
    )(*args)


def _s5_kernel(u_ref, z_ref, wb_ref, wc_ref, apr_ref, api_ref, d_ref, gw_ref, gb_ref,
               sr0_ref, si0_ref, y_ref, sr_ref, si_ref, xr_scr, xi_scr, *, pad_rows):
    c = u_ref.shape[0]
    n_tiles = xr_scr.shape[0]
    tiles_per_oct = n_tiles // S5_OCT
    half = tiles_per_oct * LANES

    @pl.when(pl.program_id(1) == 0)
    def _():
        sr_ref[...] = sr0_ref[...]
        si_ref[...] = si0_ref[...]

    u = u_ref[...]
    if pad_rows:
        valid = lax.broadcasted_iota(jnp.int32, (c, 1), 0) >= pad_rows
        u = jnp.where(valid, u, 0.0)

    for k in range(S5_OCT):
        bu = _dot(u[:, k * LANES:(k + 1) * LANES], wb_ref[k])
        for q in range(tiles_per_oct):
            xr_scr[k * tiles_per_oct + q] = bu[:, q * LANES:(q + 1) * LANES]
            xi_scr[k * tiles_per_oct + q] = bu[:, half + q * LANES:half + (q + 1) * LANES]

    zeros8 = jnp.zeros((SUBLANES, LANES), F32)

    def scan_tile(j, carry):
        xr = xr_scr[j]
        xi = xi_scr[j]
        for d in (1, 2, 4):
            ar = apr_ref[j, d - 1:d, :]
            ai = api_ref[j, d - 1:d, :]
            pr = _shift_rows(xr, d, zeros8)
            pi = _shift_rows(xi, d, zeros8)
            xr, xi = xr + ar * pr - ai * pi, xi + ar * pi + ai * pr
        a8r = apr_ref[j, SUBLANES - 1:SUBLANES, :]
        a8i = api_ref[j, SUBLANES - 1:SUBLANES, :]
        cr = sr_ref[j]
        ci = si_ref[j]
        p8r = apr_ref[j, 0:SUBLANES, :]
        p8i = api_ref[j, 0:SUBLANES, :]
        br = xr[0:SUBLANES] + p8r * cr - p8i * ci
        bi = xi[0:SUBLANES] + p8r * ci + p8i * cr
        xr_scr[j, 0:SUBLANES, :] = br
        xi_scr[j, 0:SUBLANES, :] = bi
        for i in range(1, c // SUBLANES):
            wr = xr[i * SUBLANES:(i + 1) * SUBLANES]
            wi = xi[i * SUBLANES:(i + 1) * SUBLANES]
            br, bi = wr + a8r * br - a8i * bi, wi + a8r * bi + a8i * br
            xr_scr[j, i * SUBLANES:(i + 1) * SUBLANES, :] = br
            xi_scr[j, i * SUBLANES:(i + 1) * SUBLANES, :] = bi
        sr_ref[j] = br[SUBLANES - 1:SUBLANES]
        si_ref[j] = bi[SUBLANES - 1:SUBLANES]
        return carry

    lax.fori_loop(0, n_tiles, scan_tile, 0)

    ys = []
    for k in range(S5_OCT):
        t0 = k * tiles_per_oct
        xk = jnp.concatenate([xr_scr[t0 + q] for q in range(tiles_per_oct)]
                             + [xi_scr[t0 + q] for q in range(tiles_per_oct)], axis=1)
        ys.append(_dot(xk, wc_ref[k]))
    y = jnp.concatenate(ys, axis=1) + d_ref[...] * u
    g = jax.nn.gelu(y)
    gate = _sigmoid(_dot(g, gw_ref[...]) + gb_ref[...])
    y_ref[...] = (g * gate * _silu(z_ref[...])).astype(y_ref.dtype)


def _s5_call(p, col_u, col_z, wts, state0, bsz, n_chunks, pad_rows):
    c = CHUNK
    width = wts["d"].shape[1]
    n_tiles = wts["apr"].shape[0]
    row = lambda b, i: b * n_chunks + i
    st_spec_in = pl.BlockSpec((None, n_tiles, 1, LANES), lambda b, i: (0, 0, 0, 0))
    st_spec_out = pl.BlockSpec((None, n_tiles, 1, LANES), lambda b, i: (b, 0, 0, 0))
    st_shape = jax.ShapeDtypeStruct((bsz, n_tiles, 1, LANES), F32)
    return pl.pallas_call(
        functools.partial(_s5_kernel, pad_rows=pad_rows),
        grid=(bsz, n_chunks),
        in_specs=[pl.BlockSpec((c, width), lambda b, i: (row(b, i), col_u)),
                  pl.BlockSpec((c, width), lambda b, i: (row(b, i), col_z)),
                  _const_spec(wts["wb"].shape), _const_spec(wts["wc"].shape),
                  _const_spec(wts["apr"].shape), _const_spec(wts["api"].shape),
                  _const_spec(wts["d"].shape), _const_spec(wts["gw"].shape),
                  _const_spec(wts["gb"].shape), st_spec_in, st_spec_in],
        out_specs=[pl.BlockSpec((c, width), lambda b, i: (row(b, i), 0)), st_spec_out, st_spec_out],
        out_shape=[jax.ShapeDtypeStruct((bsz * n_chunks * c, width), BF16), st_shape, st_shape],
        scratch_shapes=[pltpu.VMEM((n_tiles, c, LANES), F32), pltpu.VMEM((n_tiles, c, LANES), F32)],
        compiler_params=_params(2),
        name="s5_mixer",
    )(p, p, wts["wb"], wts["wc"], wts["apr"], wts["api"], wts["d"], wts["gw"], wts["gb"],
      state0[0], state0[1])


def _s5_weights(lam_re, lam_im, log_dt, b_re, b_im, c_re, c_im, d, glu_w, glu_b):
    g_n, p_n = lam_re.shape
    dt = jnp.exp(log_dt)[:, None]
    mag = jnp.exp(lam_re * dt)
    ar, ai = mag * jnp.cos(lam_im * dt), mag * jnp.sin(lam_im * dt)
    den = lam_re * lam_re + lam_im * lam_im
    qr = ((ar - 1.0) * lam_re + ai * lam_im) / den
    qi = (ai * lam_re - (ar - 1.0) * lam_im) / den
    bbr = qr[..., None] * b_re - qi[..., None] * b_im
    bbi = qr[..., None] * b_im + qi[..., None] * b_re
    n_oct = g_n // S5_OCT
    eye = jnp.eye(S5_OCT, dtype=F32)

    def in_proj(w):
        w = w.reshape(n_oct, S5_OCT, p_n, S5_GROUP_SIZE)
        return jnp.einsum("kgph,gG->kghGp", w, eye).reshape(n_oct, S5_OCT * S5_GROUP_SIZE, S5_OCT * p_n)

    def out_proj(w):
        w = w.reshape(n_oct, S5_OCT, S5_GROUP_SIZE, p_n)
        return jnp.einsum("kghp,gG->kgpGh", w, eye).reshape(n_oct, S5_OCT * p_n, S5_OCT * S5_GROUP_SIZE)

    wb = jnp.concatenate([in_proj(bbr), in_proj(bbi)], axis=2).astype(BF16)
    wc = jnp.concatenate([out_proj(c_re), out_proj(-c_im)], axis=1).astype(BF16)
    pr, pi = ar.reshape(1, -1), ai.reshape(1, -1)
    while pr.shape[0] < CHUNK:
        lr_, li_ = pr[-1:], pi[-1:]
        pr, pi = (jnp.concatenate([pr, pr * lr_ - pi * li_], axis=0),
                  jnp.concatenate([pi, pr * li_ + pi * lr_], axis=0))
    n_tiles = g_n * p_n // LANES
    tile = lambda a: a.reshape(CHUNK, n_tiles, LANES).transpose(1, 0, 2)
    width = d.shape[0]
    return dict(wb=wb, wc=wc, apr=tile(pr), api=tile(pi), d=d.reshape(1, width),
                gw=glu_w.astype(BF16), gb=glu_b.reshape(1, width))


def _log_sigmoid(x):
    return jnp.minimum(x, 0.0) - jnp.log1p(jnp.exp(-jnp.abs(x)))


def _mlstm_kernel(xb_ref, zb_ref, cw_ref, cb_ref, wqk_ref, wv_ref, wgq_ref, wgk_ref, wgv_ref,
                  bg_ref, nw_ref, sk_ref, c0_ref, n0_ref, m0_ref, t0_ref,
                  y_ref, c_ref, n_ref, m_ref, t_ref, *, pad_rows):
    c, width = xb_ref.shape
    n_heads = c_ref.shape[0]
    dh = c_ref.shape[1]
    n_blk = wqk_ref.shape[0]

    @pl.when(pl.program_id(1) == 0)
    def _():
        c_ref[...] = c0_ref[...]
        n_ref[...] = n0_ref[...]
        m_ref[...] = m0_ref[...]
        t_ref[...] = t0_ref[...]

    row_c = lax.broadcasted_iota(jnp.int32, (c, 1), 0)
    xf = xb_ref[...]
    if pad_rows:
        xf = jnp.where(row_c >= pad_rows, xf, 0.0)
    xc = _causal_conv_silu(xf, t_ref[...], cw_ref, cb_ref)
    t_ref[...] = xf[c - SUBLANES:]

    qs, ks, vs = [], [], []
    for j in range(n_blk):
        sl = slice(j * QKV_TILE, (j + 1) * QKV_TILE)
        qk = _dot(xc[:, sl], wqk_ref[j])
        qs.append(qk[:, :QKV_TILE])
        ks.append(qk[:, QKV_TILE:])
        vs.append(_dot(xf[:, sl], wv_ref[j]))
    q = jnp.concatenate(qs, axis=1)
    k = jnp.concatenate(ks, axis=1)
    v = jnp.concatenate(vs, axis=1)

    gates = _dot(q, wgq_ref[...]) + _dot(k, wgk_ref[...]) + _dot(v, wgv_ref[...]) + bg_ref[...]
    ig = gates[:, :LANES]
    lf = _log_sigmoid(gates[:, LANES:])
    if pad_rows:
        ig = jnp.where(row_c >= pad_rows, ig, -jnp.inf)
        lf = jnp.where(row_c >= pad_rows, lf, 0.0)
    r_i = lax.broadcasted_iota(jnp.int32, (c, c), 0)
    c_i = lax.broadcasted_iota(jnp.int32, (c, c), 1)
    causal = r_i >= c_i
    tril = causal.astype(F32)
    bcum = _dot_f32(tril, lf)
    lf_t = lf.T
    ig_t = ig.T
    bcum_t = _dot_f32(lf_t, (r_i <= c_i).astype(F32))
    scale = dh ** -0.5

    for h in range(n_heads):
        hs = slice(h * dh, (h + 1) * dh)
        bt = bcum[:, h:h + 1]
        bs = bcum_t[h:h + 1, :]
        igs = ig_t[h:h + 1, :]
        m_old = m_ref[h][:, 0:1]
        dmat = jnp.where(causal, bt - bs + igs, -jnp.inf)
        inter = bt + m_old
        mt = jnp.maximum(inter, jnp.max(dmat, axis=1, keepdims=True))
        wt = jnp.exp(dmat - mt)
        w_prev = jnp.exp(inter - mt)
        qh = q[:, hs] * scale
        kh = k[:, hs]
        vh = v[:, hs]
        s = _dot_nt(qh, kh) * wt
        num = _dot(s, vh) + w_prev * _dot(qh, c_ref[h])
        den = (jnp.sum(s, axis=1, keepdims=True)
               + w_prev * jnp.sum(qh * n_ref[h], axis=1, keepdims=True))
        hh = num / jnp.maximum(jnp.abs(den), jnp.exp(-mt))
        blast = bt[c - 1:c, :]
        g_row = blast - bs + igs
        m_new = jnp.maximum(blast + m_old, jnp.max(g_row, axis=1, keepdims=True))
        decay = jnp.exp(blast + m_old - m_new)
        g_col = blast - bt + ig[:, h:h + 1]
        wk = jnp.exp(g_col - m_new) * kh
        c_ref[h] = decay * c_ref[h] + _dot_tn(wk, vh)
        n_ref[h] = decay * n_ref[h] + jnp.sum(wk, axis=0, keepdims=True)
        m_ref[h] = jnp.broadcast_to(m_new, (1, LANES))
        mu = jnp.mean(hh, axis=1, keepdims=True)
        hc = hh - mu
        var = jnp.mean(hc * hc, axis=1, keepdims=True)
        hn = hc * lax.rsqrt(var + HEAD_NORM_EPS) * nw_ref[:, hs]
        out = (hn + sk_ref[:, hs] * xc[:, hs]) * _silu(zb_ref[:, hs])
        y_ref[:, hs] = out.astype(y_ref.dtype)


def _mlstm_call(p, col_x, col_z, wts, state0, bsz, n_chunks, pad_rows):
    c = CHUNK
    width = wts["cw"].shape[1]
    nh = MLSTM_HEADS
    dh = width // nh
    row = lambda b, i: b * n_chunks + i
    shapes = [(nh, dh, dh), (nh, 1, dh), (nh, 1, LANES), (SUBLANES, width)]
    st_in = [pl.BlockSpec((None,) + s, lambda b, i, nd=len(s): (0,) * (nd + 1)) for s in shapes]
    st_out = [pl.BlockSpec((None,) + s, lambda b, i, nd=len(s): (b,) + (0,) * nd) for s in shapes]
    st_shapes = [jax.ShapeDtypeStruct((bsz,) + s, F32) for s in shapes]
    names = ["cw", "cb", "wqk", "wv", "wgq", "wgk", "wgv", "bg", "nw", "sk"]
    return pl.pallas_call(
        functools.partial(_mlstm_kernel, pad_rows=pad_rows),
        grid=(bsz, n_chunks),
        in_specs=[pl.BlockSpec((c, width), lambda b, i: (row(b, i), col_x)),
                  pl.BlockSpec((c, width), lambda b, i: (row(b, i), col_z))]
        + [_const_spec(wts[n].shape) for n in names] + st_in,
        out_specs=[pl.BlockSpec((c, width), lambda b, i: (row(b, i), 0))] + st_out,
        out_shape=[jax.ShapeDtypeStruct((bsz * n_chunks * c, width), BF16)] + st_shapes,
        compiler_params=_params(2),
        name="mlstm_mixer",
    )(p, p, *[wts[n] for n in names], *state0)


def _block_diag_tiles(w, tile):
    nb, b, _ = w.shape
    per = tile // b
    w = w.reshape(nb // per, per, b, b)
    eye = jnp.eye(per, dtype=w.dtype)
    return jnp.einsum("jnio,nN->jniNo", w, eye).reshape(nb // per, tile, tile)


def _mlstm_weights(conv_w, conv_b, wq, wk, wv, w_gate, b_gate, norm_w, skip):
    width = conv_w.shape[1]
    nh = MLSTM_HEADS
    wqk = jnp.concatenate([_block_diag_tiles(wq, QKV_TILE), _block_diag_tiles(wk, QKV_TILE)],
                          axis=2).astype(BF16)
    wvt = _block_diag_tiles(wv, QKV_TILE).astype(BF16)

    def gate_w(w):
        out = jnp.zeros((width, 2 * LANES), F32)
        out = out.at[:, :nh].set(w[:, :nh]).at[:, LANES:LANES + nh].set(w[:, nh:])
        return out.astype(BF16)

    bg = jnp.zeros((1, 2 * LANES), F32)
    bg = bg.at[0, :nh].set(b_gate[:nh]).at[0, LANES:LANES + nh].set(b_gate[nh:])
    return dict(cw=conv_w, cb=conv_b.reshape(1, width), wqk=wqk, wv=wvt,
                wgq=gate_w(w_gate[:width]), wgk=gate_w(w_gate[width:2 * width]),
                wgv=gate_w(w_gate[2 * width:]), bg=bg,
                nw=norm_w.reshape(1, width), sk=skip.reshape(1, width))


def _ssd_kernel(z_ref, xs_ref, bm_ref, cm_ref, dt_ref, cwx_ref, cbx_ref, cwb_ref, cbb_ref,
                cwc_ref, cbc_ref, dtb_ref, a_ref, d_ref, gn_ref,
                s0_ref, tx0_ref, tb0_ref, tc0_ref,
                y_ref, s_ref, tx_ref, tb_ref, tc_ref, y_scr, *, pad_rows):
    c, inner = xs_ref.shape
    n_pairs = s_ref.shape[0]
    pairs_per_group = n_pairs // SSD_GROUPS
    gw = inner // SSD_GROUPS

    @pl.when(pl.program_id(1) == 0)
    def _():
        s_ref[...] = s0_ref[...]
        tx_ref[...] = tx0_ref[...]
        tb_ref[...] = tb0_ref[...]
        tc_ref[...] = tc0_ref[...]

    row_c = lax.broadcasted_iota(jnp.int32, (c, 1), 0)
    xs_in, bm_in, cm_in = xs_ref[...], bm_ref[...], cm_ref[...]
    dt = jax.nn.softplus(dt_ref[...] + dtb_ref[...])
    if pad_rows:
        valid = row_c >= pad_rows
        xs_in = jnp.where(valid, xs_in, 0.0)
        bm_in = jnp.where(valid, bm_in, 0.0)
        cm_in = jnp.where(valid, cm_in, 0.0)
        dt = jnp.where(valid, dt, 0.0)
    xs = _causal_conv_silu(xs_in, tx_ref[...], cwx_ref, cbx_ref)
    bm = _causal_conv_silu(bm_in, tb_ref[...], cwb_ref, cbb_ref)
    cm = _causal_conv_silu(cm_in, tc_ref[...], cwc_ref, cbc_ref)
    tx_ref[...] = xs_in[c - SUBLANES:]
    tb_ref[...] = bm_in[c - SUBLANES:]
    tc_ref[...] = cm_in[c - SUBLANES:]

    r_i = lax.broadcasted_iota(jnp.int32, (c, c), 0)
    c_i = lax.broadcasted_iota(jnp.int32, (c, c), 1)
    causal = r_i >= c_i
    da = dt * a_ref[...]
    cum = _dot_f32(causal.astype(F32), da)
    cum_t = _dot_f32(da.T, (r_i <= c_i).astype(F32))
    last = cum[c - 1:c, :]
    dec = jnp.exp(last - cum) * dt
    ecum = jnp.exp(cum)
    elast = jnp.exp(last)
    lane = lax.broadcasted_iota(jnp.int32, (1, LANES), 1)
    lo = lane < SSD_HEAD_DIM

    for g in range(SSD_GROUPS):
        gs = slice(g * SSD_STATE, (g + 1) * SSD_STATE)
        cg = cm[:, gs]
        bg = bm[:, gs]
        cb = _dot_nt(cg, bg)
        for pr in range(pairs_per_group):
            pidx = g * pairs_per_group + pr
            h0 = 2 * pidx
            ps = slice(pidx * LANES, (pidx + 1) * LANES)
            xp = xs[:, ps]

            def both(arr_col):
                return jnp.where(lo, arr_col[:, h0:h0 + 1], arr_col[:, h0 + 1:h0 + 2])

            xdt = xp * both(dt)
            seg0 = jnp.exp(jnp.where(causal, cum[:, h0:h0 + 1] - cum_t[h0:h0 + 1, :], -jnp.inf))
            seg1 = jnp.exp(jnp.where(causal, cum[:, h0 + 1:h0 + 2] - cum_t[h0 + 1:h0 + 2, :],
                                     -jnp.inf))
            y_diag = jnp.where(lo, _dot(cb * seg0, xdt), _dot(cb * seg1, xdt))
            st = s_ref[pidx]
            y_off = both(ecum) * _dot(cg, st)
            s_ref[pidx] = (jnp.where(lo, elast[:, h0:h0 + 1], elast[:, h0 + 1:h0 + 2]) * st
                           + _dot_tn(bg, xp * both(dec)))
            y = y_diag + y_off + d_ref[:, ps] * xp
            y_scr[:, ps] = y

    for g in range(SSD_GROUPS):
        gs = slice(g * gw, (g + 1) * gw)
        yg = y_scr[:, gs] * _silu(z_ref[:, gs])
        yg = yg * lax.rsqrt(jnp.mean(yg * yg, axis=1, keepdims=True) + NORM_EPS)
        y_ref[:, gs] = (yg * gn_ref[:, gs]).astype(y_ref.dtype)


def _ssd_call(p, pdt, wts, state0, bsz, n_chunks, pad_rows):
    c = CHUNK
    inner = wts["d"].shape[1]
    gn = wts["cwb"].shape[1]
    n_pairs = inner // LANES
    row = lambda b, i: b * n_chunks + i
    shapes = [(n_pairs, SSD_STATE, LANES), (SUBLANES, inner), (SUBLANES, gn), (SUBLANES, gn)]
    st_in = [pl.BlockSpec((None,) + s, lambda b, i, nd=len(s): (0,) * (nd + 1)) for s in shapes]
    st_out = [pl.BlockSpec((None,) + s, lambda b, i, nd=len(s): (b,) + (0,) * nd) for s in shapes]
    st_shapes = [jax.ShapeDtypeStruct((bsz,) + s, F32) for s in shapes]
    names = ["cwx", "cbx", "cwb", "cbb", "cwc", "cbc", "dtb", "a", "d", "gn"]
    bcol = 2 * inner // gn
    return pl.pallas_call(
        functools.partial(_ssd_kernel, pad_rows=pad_rows),
        grid=(bsz, n_chunks),
        in_specs=[pl.BlockSpec((c, inner), lambda b, i: (row(b, i), 0)),
                  pl.BlockSpec((c, inner), lambda b, i: (row(b, i), 1)),
                  pl.BlockSpec((c, gn), lambda b, i: (row(b, i), bcol)),
                  pl.BlockSpec((c, gn), lambda b, i: (row(b, i), bcol + 1)),
                  pl.BlockSpec((c, LANES), lambda b, i: (row(b, i), 0))]
        + [_const_spec(wts[n].shape) for n in names] + st_in,
        out_specs=[pl.BlockSpec((c, inner), lambda b, i: (row(b, i), 0))] + st_out,
        out_shape=[jax.ShapeDtypeStruct((bsz * n_chunks * c, inner), BF16)] + st_shapes,
        scratch_shapes=[pltpu.VMEM((c, inner), F32)],
        compiler_params=_params(2),
        name="ssd_mixer",
    )(p, p, p, p, pdt, *[wts[n] for n in names], *state0)


def _ssd_weights(conv_w, conv_b, dt_bias, a_log, d, gnorm):
    n_heads = dt_bias.shape[0]
    inner = gnorm.shape[0]
    gn = (conv_w.shape[1] - inner) // 2
    pad = lambda v: jnp.zeros((1, LANES), F32).at[0, :n_heads].set(v)
    cb = conv_b.reshape(1, -1)
    return dict(cwx=conv_w[:, :inner], cbx=cb[:, :inner],
                cwb=conv_w[:, inner:inner + gn], cbb=cb[:, inner:inner + gn],
                cwc=conv_w[:, inner + gn:], cbc=cb[:, inner + gn:],
                dtb=pad(dt_bias), a=pad(-jnp.exp(a_log)),
                d=jnp.repeat(d, inner // n_heads).reshape(1, inner), gn=gnorm.reshape(1, inner))


def kernel(x, meta_tokens, ab_norm, ab_w_in, s5_lambda_re, s5_lambda_im, s5_log_dt, s5_b_re, s5_b_im, s5_c_re, s5_c_im, s5_d, s5_glu_w, s5_glu_b, ml_conv_w, ml_conv_b, ml_wq, ml_wk, ml_wv, ml_w_gate, ml_b_gate, ml_norm, ml_skip, ab_w_out, ssd_norm, ssd_w_in, ssd_conv_w, ssd_conv_b, ssd_dt_bias, ssd_a_log, ssd_d, ssd_gnorm, ssd_w_out, final_norm):
    bsz, seq, d_model = x.shape
    assert seq % CHUNK == 0 and N_META <= CHUNK
    n_chunks = seq // CHUNK
    s5_w = s5_d.shape[1]
    ml_w = ml_conv_w.shape[2]
    ssd_inner = ssd_gnorm.shape[1]
    ssd_heads = ssd_dt_bias.shape[1]
    ssd_main = ssd_w_in.shape[2] - ssd_heads
    pad_rows = CHUNK - N_META

    w_in0 = ab_w_in[0]
    w_in0 = jnp.concatenate([w_in0[:, 2 * s5_w:], w_in0[:, :2 * s5_w]], axis=1).astype(BF16)
    col_u = 2 * ml_w // s5_w
    w_out0 = ab_w_out[0].astype(BF16)
    s5_wts = _s5_weights(s5_lambda_re[0], s5_lambda_im[0], s5_log_dt[0], s5_b_re[0], s5_b_im[0],
                         s5_c_re[0], s5_c_im[0], s5_d[0], s5_glu_w[0], s5_glu_b[0])
    ml_wts = _mlstm_weights(ml_conv_w[0], ml_conv_b[0], ml_wq[0], ml_wk[0], ml_wv[0],
                            ml_w_gate[0], ml_b_gate[0], ml_norm[0], ml_skip[0])
    w_in1 = ssd_w_in[0]
    w_in1_main = w_in1[:, :ssd_main].astype(BF16)
    w_in1_dt = jnp.zeros((d_model, LANES), F32).at[:, :ssd_heads].set(w_in1[:, ssd_main:]).astype(BF16)
    w_out1 = ssd_w_out[0].astype(BF16)
    ssd_wts = _ssd_weights(ssd_conv_w[0], ssd_conv_b[0], ssd_dt_bias[0], ssd_a_log[0], ssd_d[0],
                           ssd_gnorm[0])

    n_tiles = s5_wts["apr"].shape[0]
    dh = ml_w // MLSTM_HEADS
    zeros = lambda *s: jnp.zeros(s, F32)
    s5_zero = (zeros(1, n_tiles, 1, LANES), zeros(1, n_tiles, 1, LANES))
    ml_zero = (zeros(1, MLSTM_HEADS, dh, dh), zeros(1, MLSTM_HEADS, 1, dh),
               zeros(1, MLSTM_HEADS, 1, LANES), zeros(1, SUBLANES, ml_w))
    gn = (ssd_main - 2 * ssd_inner) // 2
    ssd_zero = (zeros(1, ssd_inner // LANES, SSD_STATE, LANES), zeros(1, SUBLANES, ssd_inner),
                zeros(1, SUBLANES, gn), zeros(1, SUBLANES, gn))

    def layer0(h, b, nc, pad, s5_state, ml_state):
        hn = _rmsnorm(h, ab_norm[0], BF16)
        p = _matmul([hn], [w_in0], name="ab_in_proj")
        ya, sr, si = _s5_call(p, col_u, col_u + 1, s5_wts, s5_state, b, nc, pad)
        yb, cst, nst, mst, tst = _mlstm_call(p, 0, 1, ml_wts, ml_state, b, nc, pad)
        h = _matmul([ya, yb], [w_out0[:s5_w], w_out0[s5_w:]], res=h, name="ab_out_proj")
        return h, (sr, si), (cst, nst, mst, tst)

    def layer1(h, b, nc, pad, ssd_state):
        hn = _rmsnorm(h, ssd_norm[0], BF16)
        p = _matmul([hn], [w_in1_main], name="ssd_in_proj")
        pdt = _matmul([hn], [w_in1_dt], name="ssd_dt_proj")
        y, *st = _ssd_call(p, pdt, ssd_wts, ssd_state, b, nc, pad)
        h = _matmul([y], [w_out1], res=h, name="ssd_out_proj")
        return h, tuple(st)

    hm = jnp.concatenate([zeros(pad_rows, d_model), meta_tokens.astype(F32)], axis=0)
    hm, s5_state, ml_state = layer0(hm, 1, 1, pad_rows, s5_zero, ml_zero)
    _, ssd_state = layer1(hm, 1, 1, pad_rows, ssd_zero)

    h = x.reshape(bsz * seq, d_model)
    h, _, _ = layer0(h, bsz, n_chunks, 0, s5_state, ml_state)
    h, _ = layer1(h, bsz, n_chunks, 0, ssd_state)
    out = _rmsnorm(h, final_norm, F32)
    return out.reshape(bsz, seq, d_model)
```

```python
import functools

import jax
import jax.numpy as jnp
from jax import lax
from jax.experimental import pallas as pl
from jax.experimental.pallas import tpu as pltpu

F32 = jnp.float32
BF16 = jnp.bfloat16

N_META = 16
CHUNK = 128
NORM_EPS = 1e-6
HEAD_NORM_EPS = 1e-5
LANES = 128
SUBLANES = 8
VMEM_LIMIT = 56 * 1024 * 1024

S5_GROUP_SIZE = 16
S5_STATE = 64
S5_OCT = 8
MLSTM_HEADS = 8
QKV_BLOCK = 4
QKV_TILE = 256
SSD_HEAD_DIM = 64
SSD_STATE = 128
SSD_GROUPS = 8
CONV_BLOCK = 512
LOG2E = 1.4426950408889634


def _params(n_grid):
    return pltpu.CompilerParams(dimension_semantics=("arbitrary",) * n_grid,
                                vmem_limit_bytes=VMEM_LIMIT)


def _const_spec(shape):
    nd = len(shape)
    return pl.BlockSpec(shape, lambda *_: (0,) * nd, pipeline_mode=pl.Buffered(1))


def _dot(a, b):
    return jnp.dot(a.astype(BF16), b.astype(BF16), preferred_element_type=F32)


def _dot_nt(a, b):
    return lax.dot_general(a.astype(BF16), b.astype(BF16), (((1,), (1,)), ((), ())),
                           preferred_element_type=F32)


def _dot_f32(a, b):
    return jnp.dot(a, b, preferred_element_type=F32, precision=lax.Precision.HIGHEST)


def _sigmoid(x):
    return 0.5 * (1.0 + jnp.tanh(0.5 * x))


def _silu_of_half(h):
    return h + h * jnp.tanh(h)


def _shift_rows(x, d, prev8):
    rolled = pltpu.roll(x, d, axis=0)
    row8 = lax.broadcasted_iota(jnp.int32, (SUBLANES, x.shape[1]), 0)
    head = jnp.where(row8 < d, pltpu.roll(prev8, d, axis=0), rolled[:SUBLANES])
    if x.shape[0] == SUBLANES:
        return head
    return jnp.concatenate([head, rolled[SUBLANES:]], axis=0)


def _causal_conv_silu(x, tail8, w_half, b_half):
    assert w_half.shape[0] == 4
    w0, w1, w2, w3 = (w_half[i:i + 1, :] for i in range(4))
    x2 = _shift_rows(x, 2, tail8)
    u = x * w2 + x2 * w0
    u_tail = tail8 * w2 + pltpu.roll(tail8, 2, axis=0) * w0
    h = x * w3 + x2 * w1 + b_half + _shift_rows(u, 1, u_tail)
    return h + h * jnp.tanh(h)


def _causal_masks(c):
    r_i = lax.broadcasted_iota(jnp.int32, (c, c), 0)
    c_i = lax.broadcasted_iota(jnp.int32, (c, c), 1)
    return r_i >= c_i


def _rmsnorm_kernel(x_ref, g_ref, o_ref):
    x = x_ref[...]
    y = x * lax.rsqrt(jnp.mean(x * x, axis=-1, keepdims=True) + NORM_EPS)
    o_ref[...] = (y * g_ref[...]).astype(o_ref.dtype)


def _rmsnorm(x, g, out_dtype):
    m, d = x.shape
    tm = min(m, 512)
    return pl.pallas_call(
        _rmsnorm_kernel,
        grid=(m // tm,),
        in_specs=[pl.BlockSpec((tm, d), lambda i: (i, 0)), _const_spec((1, d))],
        out_specs=pl.BlockSpec((tm, d), lambda i: (i, 0)),
        out_shape=jax.ShapeDtypeStruct((m, d), out_dtype),
        compiler_params=_params(1),
        name="rmsnorm",
    )(x, g.reshape(1, d))


def _norm_proj_kernel(a_ref, g_ref, w_ref, *rest, has_extra):
    if has_extra:
        wx_ref, o_ref, ox_ref, a_scr = rest
    else:
        o_ref, a_scr = rest
    tm = a_ref.shape[0]
    slab = min(tm, CHUNK)

    @pl.when(pl.program_id(1) == 0)
    def _():
        def norm_slab(i, carry):
            rows = pl.ds(pl.multiple_of(i * slab, slab), slab)
            x = a_ref[rows, :]
            y = x * lax.rsqrt(jnp.mean(x * x, axis=-1, keepdims=True) + NORM_EPS)
            a_scr[rows, :] = (y * g_ref[...]).astype(BF16)
            return carry

        lax.fori_loop(0, tm // slab, norm_slab, 0)
        if has_extra:
            ox_ref[...] = jnp.dot(a_scr[...], wx_ref[...], preferred_element_type=F32)

    o_ref[...] = jnp.dot(a_scr[...], w_ref[...], preferred_element_type=F32)


def _norm_proj(a, g, w, w_extra=None, name="norm_proj"):
    m, k = a.shape
    n = w.shape[1]
    tm = min(m, 1024)
    tn = 1024
    in_specs = [pl.BlockSpec((tm, k), lambda i, j: (i, 0)), _const_spec((1, k)),
                pl.BlockSpec((k, tn), lambda i, j: (0, j))]
    out_specs = [pl.BlockSpec((tm, tn), lambda i, j: (i, j))]
    out_shape = [jax.ShapeDtypeStruct((m, n), F32)]
    args = [a, g.reshape(1, k), w]
    if w_extra is not None:
        nx = w_extra.shape[1]
        in_specs.append(_const_spec((k, nx)))
        out_specs.append(pl.BlockSpec((tm, nx), lambda i, j: (i, 0)))
        out_shape.append(jax.ShapeDtypeStruct((m, nx), F32))
        args.append(w_extra)
    return pl.pallas_call(
        functools.partial(_norm_proj_kernel, has_extra=w_extra is not None),
        grid=(m // tm, n // tn),
        in_specs=in_specs, out_specs=out_specs, out_shape=out_shape,
        scratch_shapes=[pltpu.VMEM((tm, k), BF16)],
        compiler_params=_params(2),
        name=name,
    )(*args)


def _res_proj_kernel(a_ref, w_ref, r_ref, o_ref):
    o_ref[...] = jnp.dot(a_ref[...], w_ref[...], preferred_element_type=F32) + r_ref[...]


def _res_proj(a, w, res, name="res_proj"):
    m, k = a.shape
    n = w.shape[1]
    tm = min(m, 1024)
    tn = 1024
    return pl.pallas_call(
        _res_proj_kernel,
        grid=(m // tm, n // tn),
        in_specs=[pl.BlockSpec((tm, k), lambda i, j: (i, 0)),
                  pl.BlockSpec((k, tn), lambda i, j: (0, j)),
                  pl.BlockSpec((tm, tn), lambda i, j: (i, j))],
        out_specs=pl.BlockSpec((tm, tn), lambda i, j: (i, j)),
        out_shape=jax.ShapeDtypeStruct((m, n), F32),
        compiler_params=_params(2),
        name=name,
    )(a, w, res)


def _s5_kernel(u_ref, z_ref, wb_ref, wc_ref, apr_ref, api_ref, d_ref, gw_ref, gb_ref,
               sr0_ref, si0_ref, yin_ref, y_ref, sr_ref, si_ref, xr_scr, xi_scr, out_scr, *, pad_rows):
    del yin_ref
    c = u_ref.shape[0]
    seg = c // SUBLANES
    n_tiles = xr_scr.shape[0]
    tiles_per_oct = n_tiles // S5_OCT
    half = tiles_per_oct * LANES

    @pl.when(pl.program_id(1) == 0)
    def _():
        sr_ref[...] = sr0_ref[...]
        si_ref[...] = si0_ref[...]

    n_slabs = out_scr.shape[0]

    def load_permuted(ref):
        for k in range(n_slabs):
            out_scr[k] = ref[:, k * LANES:(k + 1) * LANES]
        return jnp.concatenate(
            [jnp.concatenate([out_scr[k, pl.ds(i, SUBLANES, stride=seg), :] for i in range(seg)], axis=0)
             for k in range(n_slabs)], axis=1)

    u = load_permuted(u_ref)
    if pad_rows:
        row = lax.broadcasted_iota(jnp.int32, (c, 1), 0)
        t_of_row = (row & (SUBLANES - 1)) * seg + (row >> 3)
        u = jnp.where(t_of_row >= pad_rows, u, 0.0)

    for k in range(S5_OCT):
        bu = _dot(u[:, k * LANES:(k + 1) * LANES], wb_ref[k])
        for q in range(tiles_per_oct):
            xr_scr[k * tiles_per_oct + q] = bu[:, q * LANES:(q + 1) * LANES]
            xi_scr[k * tiles_per_oct + q] = bu[:, half + q * LANES:half + (q + 1) * LANES]

    zeros8 = jnp.zeros((SUBLANES, LANES), F32)

    def cmul_add(xr, xi, pr, pi, yr, yi):
        return xr + pr * yr - pi * yi, xi + pr * yi + pi * yr

    def scan_tile(j, carry):
        blk = lambda i: slice(i * SUBLANES, (i + 1) * SUBLANES)
        a1r, a1i = apr_ref[j, 0:1, :], api_ref[j, 0:1, :]
        br, bi = xr_scr[j, blk(0), :], xi_scr[j, blk(0), :]
        loc = [(br, bi)]
        for i in range(1, seg):
            br, bi = cmul_add(xr_scr[j, blk(i), :], xi_scr[j, blk(i), :], a1r, a1i, br, bi)
            loc.append((br, bi))
        fr, fi = br, bi
        for d in (1, 2, 4):
            p = seg * d - 1
            fr, fi = cmul_add(fr, fi, apr_ref[j, p:p + 1, :], api_ref[j, p:p + 1, :],
                              _shift_rows(fr, d, zeros8), _shift_rows(fi, d, zeros8))
        cr, ci = sr_ref[j], si_ref[j]
        fr, fi = cmul_add(fr, fi, apr_ref[j, pl.ds(seg - 1, SUBLANES, stride=seg), :],
                          api_ref[j, pl.ds(seg - 1, SUBLANES, stride=seg), :], cr, ci)
        gr = _shift_rows(fr, 1, jnp.broadcast_to(cr, (SUBLANES, LANES)))
        gi = _shift_rows(fi, 1, jnp.broadcast_to(ci, (SUBLANES, LANES)))
        for i in range(seg):
            xr, xi = cmul_add(loc[i][0], loc[i][1], apr_ref[j, i:i + 1, :], api_ref[j, i:i + 1, :],
                              gr, gi)
            xr_scr[j, blk(i), :] = xr
            xi_scr[j, blk(i), :] = xi
        sr_ref[j] = fr[SUBLANES - 1:SUBLANES]
        si_ref[j] = fi[SUBLANES - 1:SUBLANES]
        return carry

    lax.fori_loop(0, n_tiles, scan_tile, 0, unroll=2)

    ys = []
    for k in range(S5_OCT):
        t0 = k * tiles_per_oct
        xk = jnp.concatenate([xr_scr[t0 + q] for q in range(tiles_per_oct)]
                             + [xi_scr[t0 + q] for q in range(tiles_per_oct)], axis=1)
        ys.append(_dot(xk, wc_ref[k]))
    y = jnp.concatenate(ys, axis=1) + d_ref[...] * u
    g = jax.nn.gelu(y)
    gate = _sigmoid(_dot(g, gw_ref[...]) + gb_ref[...])
    res = g * gate * _silu_of_half(load_permuted(z_ref))
    for k in range(n_slabs):
        for i in range(seg):
            out_scr[k, pl.ds(i, SUBLANES, stride=seg), :] = res[i * SUBLANES:(i + 1) * SUBLANES,
                                                                k * LANES:(k + 1) * LANES]
        y_ref[:, k * LANES:(k + 1) * LANES] = out_scr[k].astype(y_ref.dtype)


def _s5_call(p, col_u, col_z, y_buf, col_y, wts, state0, bsz, n_chunks, pad_rows):
    c = CHUNK
    width = wts["d"].shape[1]
    n_tiles = wts["apr"].shape[0]
    row = lambda b, i: b * n_chunks + i
    st_spec_in = pl.BlockSpec((None, n_tiles, 1, LANES), lambda b, i: (0, 0, 0, 0),
                              pipeline_mode=pl.Buffered(1))
    st_spec_out = pl.BlockSpec((None, n_tiles, 1, LANES), lambda b, i: (b, 0, 0, 0))
    st_shape = jax.ShapeDtypeStruct((bsz, n_tiles, 1, LANES), F32)
    names = ["wb", "wc", "apr", "api", "d", "gw", "gb"]
    return pl.pallas_call(
        functools.partial(_s5_kernel, pad_rows=pad_rows),
        grid=(bsz, n_chunks),
        in_specs=[pl.BlockSpec((c, width), lambda b, i: (row(b, i), col_u)),
                  pl.BlockSpec((c, width), lambda b, i: (row(b, i), col_z))]
        + [_const_spec(wts[n].shape) for n in names]
        + [st_spec_in, st_spec_in, pl.BlockSpec(memory_space=pl.ANY)],
        out_specs=[pl.BlockSpec((c, width), lambda b, i: (row(b, i), col_y)), st_spec_out, st_spec_out],
        out_shape=[jax.ShapeDtypeStruct(y_buf.shape, y_buf.dtype), st_shape, st_shape],
        scratch_shapes=[pltpu.VMEM((n_tiles, c, LANES), F32), pltpu.VMEM((n_tiles, c, LANES), F32),
                        pltpu.VMEM((width // LANES, c, LANES), F32)],
        input_output_aliases={len(names) + 4: 0},
        compiler_params=_params(2),
        name="s5_mixer",
    )(p, p, *[wts[n] for n in names], state0[0], state0[1], y_buf)


def _s5_weights(lam_re, lam_im, log_dt, b_re, b_im, c_re, c_im, d, glu_w, glu_b):
    g_n, p_n = lam_re.shape
    dt = jnp.exp(log_dt)[:, None]
    mag = jnp.exp(lam_re * dt)
    ar, ai = mag * jnp.cos(lam_im * dt), mag * jnp.sin(lam_im * dt)
    den = lam_re * lam_re + lam_im * lam_im
    qr = ((ar - 1.0) * lam_re + ai * lam_im) / den
    qi = (ai * lam_re - (ar - 1.0) * lam_im) / den
    bbr = qr[..., None] * b_re - qi[..., None] * b_im
    bbi = qr[..., None] * b_im + qi[..., None] * b_re
    n_oct = g_n // S5_OCT
    eye = jnp.eye(S5_OCT, dtype=F32)

    def in_proj(w):
        w = w.reshape(n_oct, S5_OCT, p_n, S5_GROUP_SIZE)
        return jnp.einsum("kgph,gG->kghGp", w, eye).reshape(n_oct, S5_OCT * S5_GROUP_SIZE, S5_OCT * p_n)

    def out_proj(w):
        w = w.reshape(n_oct, S5_OCT, S5_GROUP_SIZE, p_n)
        return jnp.einsum("kghp,gG->kgpGh", w, eye).reshape(n_oct, S5_OCT * p_n, S5_OCT * S5_GROUP_SIZE)

    wb = jnp.concatenate([in_proj(bbr), in_proj(bbi)], axis=2).astype(BF16)
    wc = jnp.concatenate([out_proj(c_re), out_proj(-c_im)], axis=1).astype(BF16)
    pr, pi = ar.reshape(1, -1), ai.reshape(1, -1)
    while pr.shape[0] < CHUNK:
        lr_, li_ = pr[-1:], pi[-1:]
        pr, pi = (jnp.concatenate([pr, pr * lr_ - pi * li_], axis=0),
                  jnp.concatenate([pi, pr * li_ + pi * lr_], axis=0))
    n_tiles = g_n * p_n // LANES
    tile = lambda a: a.reshape(CHUNK, n_tiles, LANES).transpose(1, 0, 2)
    width = d.shape[0]
    return dict(wb=wb, wc=wc, apr=tile(pr), api=tile(pi), d=d.reshape(1, width),
                gw=glu_w.astype(BF16), gb=glu_b.reshape(1, width))


def _log_sigmoid(x):
    return jnp.minimum(x, 0.0) - jnp.log1p(jnp.exp(-jnp.abs(x)))


def _cummax_rows(x):
    c = x.shape[0]
    neg8 = jnp.full((SUBLANES, x.shape[1]), -jnp.inf, F32)
    d = 1
    while d < c:
        if d < SUBLANES:
            sh = _shift_rows(x, d, neg8)
        else:
            sh = jnp.concatenate([jnp.full((d, x.shape[1]), -jnp.inf, F32), x[:c - d]], axis=0)
        x = jnp.maximum(x, sh)
        d *= 2
    return x


def _mlstm_kernel(*refs, pad_rows, emit_state):
    (xb_ref, zb_ref, cw_ref, cb_ref, wq_ref, wkt_ref, wv_ref, bg_ref, nw_ref, sk_ref,
     s0_ref, m0_ref, t0_ref, y_ref) = refs[:14]
    if emit_state:
        s_st, m_st, t_st, q_scr, kt_scr, v_scr, xc_scr = refs[14:]
    else:
        q_scr, kt_scr, v_scr, xc_scr, s_st, m_st, t_st = refs[14:]
    c = xb_ref.shape[0]
    n_heads, dh = s_st.shape[0], s_st.shape[1]
    n_blk = wq_ref.shape[0]
    scale = dh ** -0.5

    @pl.when(pl.program_id(1) == 0)
    def _():
        s_st[...] = s0_ref[...]
        m_st[...] = m0_ref[...]
        t_st[...] = t0_ref[...]

    row_c = lax.broadcasted_iota(jnp.int32, (c, 1), 0)
    lane = lax.broadcasted_iota(jnp.int32, (1, LANES), 1)

    gates = jnp.zeros((c, 2 * LANES), F32) + bg_ref[...]
    for j in range(n_blk):
        sl = slice(j * QKV_TILE, (j + 1) * QKV_TILE)
        xf = xb_ref[:, sl]
        if pad_rows:
            xf = jnp.where(row_c >= pad_rows, xf, 0.0)
        xc = _causal_conv_silu(xf, t_st[:, sl], cw_ref[:, sl], cb_ref[:, sl])
        t_st[:, sl] = xf[c - SUBLANES:]
        xcb = xc.astype(BF16)
        qg = jnp.dot(xcb, wq_ref[j], preferred_element_type=F32)
        vg = jnp.dot(xf.astype(BF16), wv_ref[j], preferred_element_type=F32)
        q_scr[:, sl] = qg[:, :QKV_TILE] * scale
        kt_scr[sl, :] = _dot_nt(wkt_ref[j], xcb)
        v_scr[:, sl] = vg[:, :QKV_TILE].astype(BF16)
        xc_scr[:, sl] = xc
        gates = gates + qg[:, QKV_TILE:] + vg[:, QKV_TILE:]

    ig = gates[:, :LANES]
    lf = _log_sigmoid(gates[:, LANES:])
    if pad_rows:
        ig = jnp.where(row_c >= pad_rows, ig, -jnp.inf)
        lf = jnp.where(row_c >= pad_rows, lf, 0.0)
    causal = _causal_masks(c)
    bcum = _dot_f32(causal.astype(F32), lf)
    a = ig - bcum
    m_old = m_st[...]
    big_m = jnp.maximum(_cummax_rows(a), m_old)
    w_prev = jnp.exp(m_old - big_m)
    e_inv = jnp.exp(-(bcum + big_m))
    m_last = big_m[c - 1:c]
    decay = jnp.exp(m_old - m_last)
    e_col = jnp.exp(a - m_last)
    m_st[...] = bcum[c - 1:c] + m_last
    a2_t = (a * LOG2E).T
    m2 = big_m * LOG2E
    e_t = e_col.T

    for h in range(n_heads):
        hs = slice(h * dh, (h + 1) * dh)
        wt = jnp.exp2(jnp.where(causal, a2_t[h:h + 1, :] - m2[:, h:h + 1], -jnp.inf))
        kt = kt_scr[hs, :]
        qh = q_scr[:, hs]
        s = _dot(qh, kt) * wt
        lhs = jnp.concatenate([s.astype(BF16), (qh * w_prev[:, h:h + 1]).astype(BF16)], axis=1)
        ones_h = jnp.broadcast_to(jnp.where(lane == h, 1.0, 0.0).astype(BF16), (c, LANES))
        vext = jnp.concatenate([v_scr[:, hs], ones_h], axis=1)
        st = s_st[h]
        nd = jnp.dot(lhs, jnp.concatenate([vext, st.astype(BF16)], axis=0),
                     preferred_element_type=F32)
        r_blk = 1.0 / jnp.maximum(jnp.abs(nd[:, dh:]), e_inv)
        hh = nd[:, :dh] * r_blk[:, h:h + 1]
        wkt = (kt * e_t[h:h + 1, :]).astype(BF16)
        s_st[h] = decay[:, h:h + 1] * st + jnp.dot(wkt, vext, preferred_element_type=F32)
        mu = jnp.mean(hh, axis=1, keepdims=True)
        hc = hh - mu
        var = jnp.mean(hc * hc, axis=1, keepdims=True)
        hn = hc * lax.rsqrt(var + HEAD_NORM_EPS) * nw_ref[:, hs]
        out = (hn + sk_ref[:, hs] * xc_scr[:, hs]) * _silu_of_half(zb_ref[:, hs])
        y_ref[:, hs] = out.astype(y_ref.dtype)


def _state_specs(shapes, emit_state):
    st_in = [pl.BlockSpec((None,) + s, lambda b, i, nd=len(s): (0,) * (nd + 1),
                          pipeline_mode=pl.Buffered(1)) for s in shapes]
    if emit_state:
        st_out = [pl.BlockSpec((None,) + s, lambda b, i, nd=len(s): (b,) + (0,) * nd) for s in shapes]
        return st_in, st_out, []
    return st_in, [], [pltpu.VMEM(s, F32) for s in shapes]


def _mlstm_call(p, col_x, col_z, out_width, wts, state0, bsz, n_chunks, pad_rows, emit_state):
    c = CHUNK
    width = wts["cw"].shape[1]
    nh = MLSTM_HEADS
    dh = width // nh
    row = lambda b, i: b * n_chunks + i
    shapes = [(nh, dh, dh + LANES), (1, LANES), (SUBLANES, width)]
    st_in, st_out, st_scr = _state_specs(shapes, emit_state)
    names = ["cw", "cb", "wq", "wkt", "wv", "bg", "nw", "sk"]
    work = [pltpu.VMEM((c, width), F32), pltpu.VMEM((width, c), F32),
            pltpu.VMEM((c, width), BF16), pltpu.VMEM((c, width), F32)]
    return pl.pallas_call(
        functools.partial(_mlstm_kernel, pad_rows=pad_rows, emit_state=emit_state),
        grid=(bsz, n_chunks),
        in_specs=[pl.BlockSpec((c, width), lambda b, i: (row(b, i), col_x)),
                  pl.BlockSpec((c, width), lambda b, i: (row(b, i), col_z))]
        + [_const_spec(wts[n].shape) for n in names] + st_in,
        out_specs=[pl.BlockSpec((c, width), lambda b, i: (row(b, i), 0))] + st_out,
        out_shape=[jax.ShapeDtypeStruct((bsz * n_chunks * c, out_width), BF16)]
        + ([jax.ShapeDtypeStruct((bsz,) + s, F32) for s in shapes] if emit_state else []),
        scratch_shapes=work + st_scr,
        compiler_params=_params(2),
        name="mlstm_mixer",
    )(p, p, *[wts[n] for n in names], *state0)


def _block_diag_tiles(w, tile):
    nb, b, _ = w.shape
    per = tile // b
    w = w.reshape(nb // per, per, b, b)
    eye = jnp.eye(per, dtype=w.dtype)
    return jnp.einsum("jnio,nN->jniNo", w, eye).reshape(nb // per, tile, tile)


def _mlstm_weights(conv_w, conv_b, wq, wk, wv, w_gate, b_gate, norm_w, skip):
    width = conv_w.shape[1]
    nh = MLSTM_HEADS
    n_blk = width // QKV_TILE

    def gate_cols(w_hw, w_g):
        g = jnp.einsum("nio,nog->nig", w_hw, w_g.reshape(-1, QKV_BLOCK, 2 * nh)).reshape(width, 2 * nh)
        out = jnp.zeros((width, 2 * LANES), F32)
        out = out.at[:, :nh].set(g[:, :nh]).at[:, LANES:LANES + nh].set(g[:, nh:])
        return out.reshape(n_blk, QKV_TILE, 2 * LANES)

    g_qk = gate_cols(wq, w_gate[:width]) + gate_cols(wk, w_gate[width:2 * width])
    g_v = gate_cols(wv, w_gate[2 * width:])
    wq_t = jnp.concatenate([_block_diag_tiles(wq, QKV_TILE), g_qk], axis=2).astype(BF16)
    wv_t = jnp.concatenate([_block_diag_tiles(wv, QKV_TILE), g_v], axis=2).astype(BF16)
    wk_t = _block_diag_tiles(wk, QKV_TILE).transpose(0, 2, 1).astype(BF16)
    bg = jnp.zeros((1, 2 * LANES), F32)
    bg = bg.at[0, :nh].set(b_gate[:nh]).at[0, LANES:LANES + nh].set(b_gate[nh:])
    return dict(cw=0.5 * conv_w, cb=0.5 * conv_b.reshape(1, width), wq=wq_t, wkt=wk_t, wv=wv_t, bg=bg,
                nw=norm_w.reshape(1, width), sk=skip.reshape(1, width))


def _ssd_kernel(*refs, pad_rows, emit_state):
    (z_ref, xs_ref, bm_ref, cm_ref, dt_ref, cwx_ref, cbx_ref, cwb_ref, cbb_ref, cwc_ref, cbc_ref,
     dtb_ref, a_ref, d_ref, gn_ref, s0_ref, tx0_ref, tb0_ref, tc0_ref, y_ref) = refs[:20]
    if emit_state:
        s_st, tx_st, tb_st, tc_st, xs_scr, bm_scr, cm_scr, y_scr = refs[20:]
    else:
        xs_scr, bm_scr, cm_scr, y_scr, s_st, tx_st, tb_st, tc_st = refs[20:]
    c, inner = xs_ref.shape
    n_pairs = s_st.shape[0]
    pairs_per_group = n_pairs // SSD_GROUPS
    gw = inner // SSD_GROUPS

    @pl.when(pl.program_id(1) == 0)
    def _():
        s_st[...] = s0_ref[...]
        tx_st[...] = tx0_ref[...]
        tb_st[...] = tb0_ref[...]
        tc_st[...] = tc0_ref[...]

    row_c = lax.broadcasted_iota(jnp.int32, (c, 1), 0)

    def conv_into(dst_scr, src_ref, tail_st, w_ref, b_ref):
        for j in range(src_ref.shape[1] // CONV_BLOCK):
            sl = slice(j * CONV_BLOCK, (j + 1) * CONV_BLOCK)
            x = src_ref[:, sl]
            if pad_rows:
                x = jnp.where(row_c >= pad_rows, x, 0.0)
            dst_scr[:, sl] = _causal_conv_silu(x, tail_st[:, sl], w_ref[:, sl], b_ref[:, sl])
            tail_st[:, sl] = x[c - SUBLANES:]

    conv_into(xs_scr, xs_ref, tx_st, cwx_ref, cbx_ref)
    conv_into(bm_scr, bm_ref, tb_st, cwb_ref, cbb_ref)
    conv_into(cm_scr, cm_ref, tc_st, cwc_ref, cbc_ref)

    dt = jax.nn.softplus(dt_ref[...] + dtb_ref[...])
    if pad_rows:
        dt = jnp.where(row_c >= pad_rows, dt, 0.0)
    causal = _causal_masks(c)
    cum = _dot_f32(causal.astype(F32), dt * a_ref[...])
    cum_t = cum.T
    dt_t = dt.T
    dec_t = jnp.exp(cum_t[:, c - 1:c] - cum_t) * dt_t
    src2_t = (cum_t - jnp.log(dt_t)) * LOG2E
    cum2 = cum * LOG2E
    e_last = jnp.exp(cum[c - 1:c, :])
    lo = lax.broadcasted_iota(jnp.int32, (1, LANES), 1) < SSD_HEAD_DIM

    for g in range(SSD_GROUPS):
        gs = slice(g * SSD_STATE, (g + 1) * SSD_STATE)
        cg = cm_scr[:, gs]
        bg_t = bm_scr[:, gs].T
        cb = _dot(cg, bg_t)
        for pr in range(pairs_per_group):
            pidx = g * pairs_per_group + pr
            ps = slice(pidx * LANES, (pidx + 1) * LANES)
            xp = xs_scr[:, ps]
            xpb = xp.astype(BF16)
            st = s_st[pidx]
            rhs = jnp.concatenate([xpb, st.astype(BF16)], axis=0)
            ys, ups = [], []
            for h in (2 * pidx, 2 * pidx + 1):
                bc = jnp.broadcast_to(cum2[:, h:h + 1], (c, c))
                seg = jnp.exp2(jnp.where(causal, bc - src2_t[h:h + 1, :], -jnp.inf))
                lhs = jnp.concatenate([(cb * seg).astype(BF16), (cg * jnp.exp2(bc)).astype(BF16)],
                                      axis=1)
                ys.append(jnp.dot(lhs, rhs, preferred_element_type=F32))
                ups.append(jnp.dot((bg_t * dec_t[h:h + 1, :]).astype(BF16), xpb,
                                   preferred_element_type=F32))
            h0 = 2 * pidx
            y_scr[:, ps] = jnp.where(lo, ys[0], ys[1]) + d_ref[:, ps] * xp
            el = jnp.where(lo, e_last[:, h0:h0 + 1], e_last[:, h0 + 1:h0 + 2])
            s_st[pidx] = el * st + jnp.where(lo, ups[0], ups[1])

    for g in range(SSD_GROUPS):
        gs = slice(g * gw, (g + 1) * gw)
        yg = y_scr[:, gs] * _silu_of_half(z_ref[:, gs])
        yg = yg * lax.rsqrt(jnp.mean(yg * yg, axis=1, keepdims=True) + NORM_EPS)
        y_ref[:, gs] = (yg * gn_ref[:, gs]).astype(y_ref.dtype)


def _ssd_call(p, pdt, wts, state0, bsz, n_chunks, pad_rows, emit_state):
    c = CHUNK
    inner = wts["d"].shape[1]
    gn = wts["cwb"].shape[1]
    n_pairs = inner // LANES
    row = lambda b, i: b * n_chunks + i
    shapes = [(n_pairs, SSD_STATE, LANES), (SUBLANES, inner), (SUBLANES, gn), (SUBLANES, gn)]
    st_in, st_out, st_scr = _state_specs(shapes, emit_state)
    names = ["cwx", "cbx", "cwb", "cbb", "cwc", "cbc", "dtb", "a", "d", "gn"]
    bcol = 2 * inner // gn
    work = [pltpu.VMEM((c, inner), F32), pltpu.VMEM((c, gn), F32), pltpu.VMEM((c, gn), F32),
            pltpu.VMEM((c, inner), F32)]
    return pl.pallas_call(
        functools.partial(_ssd_kernel, pad_rows=pad_rows, emit_state=emit_state),
        grid=(bsz, n_chunks),
        in_specs=[pl.BlockSpec((c, inner), lambda b, i: (row(b, i), 0)),
                  pl.BlockSpec((c, inner), lambda b, i: (row(b, i), 1)),
                  pl.BlockSpec((c, gn), lambda b, i: (row(b, i), bcol)),
                  pl.BlockSpec((c, gn), lambda b, i: (row(b, i), bcol + 1)),
                  pl.BlockSpec((c, LANES), lambda b, i: (row(b, i), 0))]
        + [_const_spec(wts[n].shape) for n in names] + st_in,
        out_specs=[pl.BlockSpec((c, inner), lambda b, i: (row(b, i), 0))] + st_out,
        out_shape=[jax.ShapeDtypeStruct((bsz * n_chunks * c, inner), BF16)]
        + ([jax.ShapeDtypeStruct((bsz,) + s, F32) for s in shapes] if emit_state else []),
        scratch_shapes=work + st_scr,
        compiler_params=_params(2),
        name="ssd_mixer",
    )(p, p, p, p, pdt, *[wts[n] for n in names], *state0)


def _ssd_weights(conv_w, conv_b, dt_bias, a_log, d, gnorm):
    n_heads = dt_bias.shape[0]
    inner = gnorm.shape[0]
    gn = (conv_w.shape[1] - inner) // 2
    pad = lambda v: jnp.zeros((1, LANES), F32).at[0, :n_heads].set(v)
    cb = 0.5 * conv_b.reshape(1, -1)
    conv_w = 0.5 * conv_w
    return dict(cwx=conv_w[:, :inner], cbx=cb[:, :inner],
                cwb=conv_w[:, inner:inner + gn], cbb=cb[:, inner:inner + gn],
                cwc=conv_w[:, inner + gn:], cbc=cb[:, inner + gn:],
                dtb=pad(dt_bias), a=pad(-jnp.exp(a_log)),
                d=jnp.repeat(d, inner // n_heads).reshape(1, inner), gn=gnorm.reshape(1, inner))


def kernel(x, meta_tokens, ab_norm, ab_w_in, s5_lambda_re, s5_lambda_im, s5_log_dt, s5_b_re, s5_b_im, s5_c_re, s5_c_im, s5_d, s5_glu_w, s5_glu_b, ml_conv_w, ml_conv_b, ml_wq, ml_wk, ml_wv, ml_w_gate, ml_b_gate, ml_norm, ml_skip, ab_w_out, ssd_norm, ssd_w_in, ssd_conv_w, ssd_conv_b, ssd_dt_bias, ssd_a_log, ssd_d, ssd_gnorm, ssd_w_out, final_norm):
    bsz, seq, d_model = x.shape
    assert seq % CHUNK == 0 and N_META <= CHUNK
    n_chunks = seq // CHUNK
    s5_w = s5_d.shape[1]
    ml_w = ml_conv_w.shape[2]
    ssd_inner = ssd_gnorm.shape[1]
    ssd_heads = ssd_dt_bias.shape[1]
    ssd_main = ssd_w_in.shape[2] - ssd_heads
    pad_rows = CHUNK - N_META

    w_in0 = ab_w_in[0]
    w_in0 = jnp.concatenate([w_in0[:, 2 * s5_w:2 * s5_w + ml_w], 0.5 * w_in0[:, 2 * s5_w + ml_w:],
                             w_in0[:, :s5_w], 0.5 * w_in0[:, s5_w:2 * s5_w]], axis=1).astype(BF16)
    col_u = 2 * ml_w // s5_w
    w_out0 = jnp.concatenate([ab_w_out[0][s5_w:], ab_w_out[0][:s5_w]], axis=0).astype(BF16)
    s5_wts = _s5_weights(s5_lambda_re[0], s5_lambda_im[0], s5_log_dt[0], s5_b_re[0], s5_b_im[0],
                         s5_c_re[0], s5_c_im[0], s5_d[0], s5_glu_w[0], s5_glu_b[0])
    ml_wts = _mlstm_weights(ml_conv_w[0], ml_conv_b[0], ml_wq[0], ml_wk[0], ml_wv[0],
                            ml_w_gate[0], ml_b_gate[0], ml_norm[0], ml_skip[0])
    w_in1 = ssd_w_in[0]
    w_in1_main = jnp.concatenate([0.5 * w_in1[:, :ssd_inner], w_in1[:, ssd_inner:ssd_main]],
                                 axis=1).astype(BF16)
    w_in1_dt = jnp.zeros((d_model, LANES), F32).at[:, :ssd_heads].set(w_in1[:, ssd_main:]).astype(BF16)
    w_out1 = ssd_w_out[0].astype(BF16)
    ssd_wts = _ssd_weights(ssd_conv_w[0], ssd_conv_b[0], ssd_dt_bias[0], ssd_a_log[0], ssd_d[0],
                           ssd_gnorm[0])

    n_tiles = s5_wts["apr"].shape[0]
    dh = ml_w // MLSTM_HEADS
    zeros = lambda *s: jnp.zeros(s, F32)
    s5_zero = (zeros(1, n_tiles, 1, LANES), zeros(1, n_tiles, 1, LANES))
    ml_zero = (zeros(1, MLSTM_HEADS, dh, dh + LANES), zeros(1, 1, LANES), zeros(1, SUBLANES, ml_w))
    gn = (ssd_main - 2 * ssd_inner) // 2
    ssd_zero = (zeros(1, ssd_inner // LANES, SSD_STATE, LANES), zeros(1, SUBLANES, ssd_inner),
                zeros(1, SUBLANES, gn), zeros(1, SUBLANES, gn))

    def layer0(h, b, nc, pad, s5_state, ml_state, emit_state):
        p = _norm_proj(h, ab_norm[0], w_in0, name="ab_in_proj")[0]
        y, *ml_out = _mlstm_call(p, 0, 1, ml_w + s5_w, ml_wts, ml_state, b, nc, pad, emit_state)
        y, sr, si = _s5_call(p, col_u, col_u + 1, y, ml_w // s5_w, s5_wts, s5_state, b, nc, pad)
        h = _res_proj(y, w_out0, h, name="ab_out_proj")
        return h, (sr, si), tuple(ml_out)

    def layer1(h, b, nc, pad, ssd_state, emit_state):
        p, pdt = _norm_proj(h, ssd_norm[0], w_in1_main, w_in1_dt, name="ssd_in_proj")
        y, *st = _ssd_call(p, pdt, ssd_wts, ssd_state, b, nc, pad, emit_state)
        h = _res_proj(y, w_out1, h, name="ssd_out_proj")
        return h, tuple(st)

    hm = jnp.concatenate([zeros(pad_rows, d_model), meta_tokens.astype(F32)], axis=0)
    hm, s5_state, ml_state = layer0(hm, 1, 1, pad_rows, s5_zero, ml_zero, True)
    _, ssd_state = layer1(hm, 1, 1, pad_rows, ssd_zero, True)

    h = x.reshape(bsz * seq, d_model)
    h, _, _ = layer0(h, bsz, n_chunks, 0, s5_state, ml_state, False)
    h, _ = layer1(h, bsz, n_chunks, 0, ssd_state, False)
    out = _rmsnorm(h, final_norm, F32)
    return out.reshape(bsz, seq, d_model)
```

```python
import functools

import jax
import jax.numpy as jnp
from jax import lax
from jax.experimental import pallas as pl
from jax.experimental.pallas import tpu as pltpu

F32 = jnp.float32
BF16 = jnp.bfloat16

N_META = 16
CHUNK = 128
NORM_EPS = 1e-6
HEAD_NORM_EPS = 1e-5
LANES = 128
SUBLANES = 8
VMEM_LIMIT = 56 * 1024 * 1024

S5_GROUP_SIZE = 16
S5_STATE = 64
S5_OCT = 8
MLSTM_HEADS = 8
QKV_BLOCK = 4
QKV_TILE = 256
SSD_HEAD_DIM = 64
SSD_STATE = 128
SSD_GROUPS = 8
CONV_BLOCK = 512
LOG2E = 1.4426950408889634


def _params(n_grid):
    return pltpu.CompilerParams(dimension_semantics=("arbitrary",) * n_grid,
                                vmem_limit_bytes=VMEM_LIMIT)


def _const_spec(shape):
    nd = len(shape)
    return pl.BlockSpec(shape, lambda *_: (0,) * nd, pipeline_mode=pl.Buffered(1))


def _dot(a, b):
    return jnp.dot(a.astype(BF16), b.astype(BF16), preferred_element_type=F32)


def _dot_nt(a, b):
    return lax.dot_general(a.astype(BF16), b.astype(BF16), (((1,), (1,)), ((), ())),
                           preferred_element_type=F32)


def _dot_f32(a, b):
    return jnp.dot(a, b, preferred_element_type=F32, precision=lax.Precision.HIGHEST)


def _sigmoid(x):
    return 0.5 * (1.0 + jnp.tanh(0.5 * x))


def _silu_of_half(h):
    return h + h * jnp.tanh(h)


def _shift_rows(x, d, prev8):
    rolled = pltpu.roll(x, d, axis=0)
    row8 = lax.broadcasted_iota(jnp.int32, (SUBLANES, x.shape[1]), 0)
    head = jnp.where(row8 < d, pltpu.roll(prev8, d, axis=0), rolled[:SUBLANES])
    if x.shape[0] == SUBLANES:
        return head
    return jnp.concatenate([head, rolled[SUBLANES:]], axis=0)


def _causal_conv_silu(x, tail8, w_half, b_half):
    assert w_half.shape[0] == 4
    w0, w1, w2, w3 = (w_half[i:i + 1, :] for i in range(4))
    x2 = _shift_rows(x, 2, tail8)
    u = x * w2 + x2 * w0
    u_tail = tail8 * w2 + pltpu.roll(tail8, 2, axis=0) * w0
    h = x * w3 + x2 * w1 + b_half + _shift_rows(u, 1, u_tail)
    return h + h * jnp.tanh(h)


def _causal_masks(c):
    r_i = lax.broadcasted_iota(jnp.int32, (c, c), 0)
    c_i = lax.broadcasted_iota(jnp.int32, (c, c), 1)
    return r_i >= c_i


def _norm_proj_kernel(a_ref, g_ref, w_ref, *rest, has_extra, w_transposed):
    if has_extra:
        wx_ref, o_ref, ox_ref, a_scr = rest
    else:
        o_ref, a_scr = rest
    tm = a_ref.shape[0]
    slab = min(tm, CHUNK)

    def mm(lhs, rhs):
        if w_transposed:
            return lax.dot_general(lhs, rhs, (((1,), (1,)), ((), ())), preferred_element_type=F32)
        return jnp.dot(lhs, rhs, preferred_element_type=F32)

    @pl.when(pl.program_id(1) == 0)
    def _():
        def norm_slab(i, carry):
            rows = pl.ds(pl.multiple_of(i * slab, slab), slab)
            x = a_ref[rows, :]
            y = x * lax.rsqrt(jnp.mean(x * x, axis=-1, keepdims=True) + NORM_EPS)
            a_scr[rows, :] = (y * g_ref[...]).astype(BF16)
            return carry

        lax.fori_loop(0, tm // slab, norm_slab, 0)
        if has_extra:
            ox_ref[...] = mm(a_scr[...], wx_ref[...])

    o_ref[...] = mm(a_scr[...], w_ref[...])


def _norm_proj(a, g, w, w_extra=None, w_transposed=False, name="norm_proj"):
    m, k = a.shape
    n = w.shape[0] if w_transposed else w.shape[1]
    tm = min(m, 1024)
    tn = 1024
    w_spec = (pl.BlockSpec((tn, k), lambda i, j: (j, 0)) if w_transposed
              else pl.BlockSpec((k, tn), lambda i, j: (0, j)))
    in_specs = [pl.BlockSpec((tm, k), lambda i, j: (i, 0)), _const_spec((1, k)), w_spec]
    out_specs = [pl.BlockSpec((tm, tn), lambda i, j: (i, j))]
    out_shape = [jax.ShapeDtypeStruct((m, n), F32)]
    args = [a, g.reshape(1, k), w]
    if w_extra is not None:
        nx = w_extra.shape[0] if w_transposed else w_extra.shape[1]
        in_specs.append(_const_spec(w_extra.shape))
        out_specs.append(pl.BlockSpec((tm, nx), lambda i, j: (i, 0)))
        out_shape.append(jax.ShapeDtypeStruct((m, nx), F32))
        args.append(w_extra)
    return pl.pallas_call(
        functools.partial(_norm_proj_kernel, has_extra=w_extra is not None,
                          w_transposed=w_transposed),
        grid=(m // tm, n // tn),
        in_specs=in_specs, out_specs=out_specs, out_shape=out_shape,
        scratch_shapes=[pltpu.VMEM((tm, k), BF16)],
        compiler_params=_params(2),
        name=name,
    )(*args)


def _res_proj_kernel(a_ref, w_ref, r_ref, o_ref):
    o_ref[...] = jnp.dot(a_ref[...], w_ref[...], preferred_element_type=F32) + r_ref[...]


def _res_proj(a, w, res, name="res_proj"):
    m, k = a.shape
    n = w.shape[1]
    tm = min(m, 1024)
    tn = 1024
    return pl.pallas_call(
        _res_proj_kernel,
        grid=(m // tm, n // tn),
        in_specs=[pl.BlockSpec((tm, k), lambda i, j: (i, 0)),
                  pl.BlockSpec((k, tn), lambda i, j: (0, j)),
                  pl.BlockSpec((tm, tn), lambda i, j: (i, j))],
        out_specs=pl.BlockSpec((tm, tn), lambda i, j: (i, j)),
        out_shape=jax.ShapeDtypeStruct((m, n), F32),
        compiler_params=_params(2),
        name=name,
    )(a, w, res)


def _res_proj_norm_kernel(a_ref, w_ref, r_ref, g_ref, o_ref):
    tm = a_ref.shape[0]
    slab = min(tm, CHUNK)
    for s in range(tm // slab):
        rows = slice(s * slab, (s + 1) * slab)
        x = jnp.dot(a_ref[rows, :], w_ref[...], preferred_element_type=F32) + r_ref[rows, :]
        y = x * lax.rsqrt(jnp.mean(x * x, axis=-1, keepdims=True) + NORM_EPS)
        o_ref[rows, :] = y * g_ref[...]


def _res_proj_norm(a, w, res, g, name="res_proj_norm"):
    m, k = a.shape
    n = w.shape[1]
    tm = min(m, 512)
    return pl.pallas_call(
        _res_proj_norm_kernel,
        grid=(m // tm,),
        in_specs=[pl.BlockSpec((tm, k), lambda i: (i, 0)), _const_spec((k, n)),
                  pl.BlockSpec((tm, n), lambda i: (i, 0)), _const_spec((1, n))],
        out_specs=pl.BlockSpec((tm, n), lambda i: (i, 0)),
        out_shape=jax.ShapeDtypeStruct((m, n), F32),
        compiler_params=_params(1),
        name=name,
    )(a, w, res, g.reshape(1, n))


def _s5_kernel(u_ref, z_ref, wb_ref, wc_ref, apr_ref, api_ref, d_ref, gw_ref, gb_ref,
               sr0_ref, si0_ref, yin_ref, y_ref, sr_ref, si_ref, out_scr, *, pad_rows):
    del yin_ref
    c = u_ref.shape[0]
    seg = c // SUBLANES
    n_tiles = apr_ref.shape[0]
    tiles_per_oct = n_tiles // S5_OCT
    half = tiles_per_oct * LANES

    @pl.when(pl.program_id(1) == 0)
    def _():
        sr_ref[...] = sr0_ref[...]
        si_ref[...] = si0_ref[...]

    n_slabs = out_scr.shape[0]

    def load_permuted(ref):
        for k in range(n_slabs):
            out_scr[k] = ref[:, k * LANES:(k + 1) * LANES]
        return jnp.concatenate(
            [jnp.concatenate([out_scr[k, pl.ds(i, SUBLANES, stride=seg), :] for i in range(seg)], axis=0)
             for k in range(n_slabs)], axis=1)

    u = load_permuted(u_ref)
    if pad_rows:
        row = lax.broadcasted_iota(jnp.int32, (c, 1), 0)
        t_of_row = (row & (SUBLANES - 1)) * seg + (row >> 3)
        u = jnp.where(t_of_row >= pad_rows, u, 0.0)

    zeros8 = jnp.zeros((SUBLANES, LANES), F32)

    def cmul_add(xr, xi, pr, pi, yr, yi):
        return xr + pr * yr - pi * yi, xi + pr * yi + pi * yr

    def scan_tile(j, bur, bui):
        blk = lambda i: slice(i * SUBLANES, (i + 1) * SUBLANES)
        a1r, a1i = apr_ref[j, 0:1, :], api_ref[j, 0:1, :]
        br, bi = bur[blk(0)], bui[blk(0)]
        loc = [(br, bi)]
        for i in range(1, seg):
            br, bi = cmul_add(bur[blk(i)], bui[blk(i)], a1r, a1i, br, bi)
            loc.append((br, bi))
        fr, fi = br, bi
        for d in (1, 2, 4):
            p = seg * d - 1
            fr, fi = cmul_add(fr, fi, apr_ref[j, p:p + 1, :], api_ref[j, p:p + 1, :],
                              _shift_rows(fr, d, zeros8), _shift_rows(fi, d, zeros8))
        cr, ci = sr_ref[j], si_ref[j]
        fr, fi = cmul_add(fr, fi, apr_ref[j, pl.ds(seg - 1, SUBLANES, stride=seg), :],
                          api_ref[j, pl.ds(seg - 1, SUBLANES, stride=seg), :], cr, ci)
        gr = _shift_rows(fr, 1, jnp.broadcast_to(cr, (SUBLANES, LANES)))
        gi = _shift_rows(fi, 1, jnp.broadcast_to(ci, (SUBLANES, LANES)))
        xs = [cmul_add(loc[i][0], loc[i][1], apr_ref[j, i:i + 1, :], api_ref[j, i:i + 1, :], gr, gi)
              for i in range(seg)]
        sr_ref[j] = fr[SUBLANES - 1:SUBLANES]
        si_ref[j] = fi[SUBLANES - 1:SUBLANES]
        return (jnp.concatenate([x[0] for x in xs], axis=0),
                jnp.concatenate([x[1] for x in xs], axis=0))

    ys = []
    for k in range(S5_OCT):
        bu = _dot(u[:, k * LANES:(k + 1) * LANES], wb_ref[k])
        xrs, xis = [], []
        for q in range(tiles_per_oct):
            xr, xi = scan_tile(k * tiles_per_oct + q, bu[:, q * LANES:(q + 1) * LANES],
                               bu[:, half + q * LANES:half + (q + 1) * LANES])
            xrs.append(xr)
            xis.append(xi)
        ys.append(_dot(jnp.concatenate(xrs + xis, axis=1), wc_ref[k]))
    y = jnp.concatenate(ys, axis=1) + d_ref[...] * u
    g = jax.nn.gelu(y)
    gate = _sigmoid(_dot(g, gw_ref[...]) + gb_ref[...])
    res = g * gate * _silu_of_half(load_permuted(z_ref))
    for k in range(n_slabs):
        for i in range(seg):
            out_scr[k, pl.ds(i, SUBLANES, stride=seg), :] = res[i * SUBLANES:(i + 1) * SUBLANES,
                                                                k * LANES:(k + 1) * LANES]
        y_ref[:, k * LANES:(k + 1) * LANES] = out_scr[k].astype(y_ref.dtype)


def _s5_call(p, col_u, col_z, y_buf, col_y, wts, state0, bsz, n_chunks, pad_rows):
    c = CHUNK
    width = wts["d"].shape[1]
    n_tiles = wts["apr"].shape[0]
    row = lambda b, i: b * n_chunks + i
    st_spec_in = pl.BlockSpec((None, n_tiles, 1, LANES), lambda b, i: (0, 0, 0, 0),
                              pipeline_mode=pl.Buffered(1))
    st_spec_out = pl.BlockSpec((None, n_tiles, 1, LANES), lambda b, i: (b, 0, 0, 0))
    st_shape = jax.ShapeDtypeStruct((bsz, n_tiles, 1, LANES), F32)
    names = ["wb", "wc", "apr", "api", "d", "gw", "gb"]
    return pl.pallas_call(
        functools.partial(_s5_kernel, pad_rows=pad_rows),
        grid=(bsz, n_chunks),
        in_specs=[pl.BlockSpec((c, width), lambda b, i: (row(b, i), col_u)),
                  pl.BlockSpec((c, width), lambda b, i: (row(b, i), col_z))]
        + [_const_spec(wts[n].shape) for n in names]
        + [st_spec_in, st_spec_in, pl.BlockSpec(memory_space=pl.ANY)],
        out_specs=[pl.BlockSpec((c, width), lambda b, i: (row(b, i), col_y)), st_spec_out, st_spec_out],
        out_shape=[jax.ShapeDtypeStruct(y_buf.shape, y_buf.dtype), st_shape, st_shape],
        scratch_shapes=[pltpu.VMEM((width // LANES, c, LANES), F32)],
        input_output_aliases={len(names) + 4: 0},
        compiler_params=_params(2),
        name="s5_mixer",
    )(p, p, *[wts[n] for n in names], state0[0], state0[1], y_buf)


def _s5_weights(lam_re, lam_im, log_dt, b_re, b_im, c_re, c_im, d, glu_w, glu_b):
    g_n, p_n = lam_re.shape
    dt = jnp.exp(log_dt)[:, None]
    mag = jnp.exp(lam_re * dt)
    ar, ai = mag * jnp.cos(lam_im * dt), mag * jnp.sin(lam_im * dt)
    den = lam_re * lam_re + lam_im * lam_im
    qr = ((ar - 1.0) * lam_re + ai * lam_im) / den
    qi = (ai * lam_re - (ar - 1.0) * lam_im) / den
    bbr = qr[..., None] * b_re - qi[..., None] * b_im
    bbi = qr[..., None] * b_im + qi[..., None] * b_re
    n_oct = g_n // S5_OCT
    eye = jnp.eye(S5_OCT, dtype=F32)

    def in_proj(w):
        w = w.reshape(n_oct, S5_OCT, p_n, S5_GROUP_SIZE)
        return jnp.einsum("kgph,gG->kghGp", w, eye).reshape(n_oct, S5_OCT * S5_GROUP_SIZE, S5_OCT * p_n)

    def out_proj(w):
        w = w.reshape(n_oct, S5_OCT, S5_GROUP_SIZE, p_n)
        return jnp.einsum("kghp,gG->kgpGh", w, eye).reshape(n_oct, S5_OCT * p_n, S5_OCT * S5_GROUP_SIZE)

    wb = jnp.concatenate([in_proj(bbr), in_proj(bbi)], axis=2).astype(BF16)
    wc = jnp.concatenate([out_proj(c_re), out_proj(-c_im)], axis=1).astype(BF16)
    pr, pi = ar.reshape(1, -1), ai.reshape(1, -1)
    while pr.shape[0] < CHUNK:
        lr_, li_ = pr[-1:], pi[-1:]
        pr, pi = (jnp.concatenate([pr, pr * lr_ - pi * li_], axis=0),
                  jnp.concatenate([pi, pr * li_ + pi * lr_], axis=0))
    n_tiles = g_n * p_n // LANES
    tile = lambda a: a.reshape(CHUNK, n_tiles, LANES).transpose(1, 0, 2)
    width = d.shape[0]
    return dict(wb=wb, wc=wc, apr=tile(pr), api=tile(pi), d=d.reshape(1, width),
                gw=glu_w.astype(BF16), gb=glu_b.reshape(1, width))


def _log_sigmoid(x):
    return jnp.minimum(x, 0.0) - jnp.log1p(jnp.exp(-jnp.abs(x)))


def _cummax_rows(x):
    c = x.shape[0]
    neg8 = jnp.full((SUBLANES, x.shape[1]), -jnp.inf, F32)
    d = 1
    while d < c:
        if d < SUBLANES:
            sh = _shift_rows(x, d, neg8)
        else:
            sh = jnp.concatenate([jnp.full((d, x.shape[1]), -jnp.inf, F32), x[:c - d]], axis=0)
        x = jnp.maximum(x, sh)
        d *= 2
    return x


def _mlstm_kernel(*refs, n_in_slabs, pad_rows, emit_state):
    xb_refs = refs[:n_in_slabs]
    zb_refs = refs[n_in_slabs:2 * n_in_slabs]
    k = 2 * n_in_slabs
    (cw_ref, cb_ref, wq_ref, wkt_ref, wv_ref, bg_ref, nw_ref, sk_ref,
     s0_ref, m0_ref, t0_ref, y_ref) = refs[k:k + 12]
    if emit_state:
        s_st, m_st, t_st, q_scr, kt_scr, v_scr, xc_scr = refs[k + 12:]
    else:
        q_scr, kt_scr, v_scr, xc_scr, s_st, m_st, t_st = refs[k + 12:]
    c, slab_w = xb_refs[0].shape
    n_heads, dh = s_st.shape[0], s_st.shape[1]
    n_blk = wq_ref.shape[0]
    y_off = y_ref.shape[1] - n_heads * dh
    scale = dh ** -0.5

    def slab_cols(slab_refs, lo, hi):
        parts = []
        while lo < hi:
            end = min(hi, (lo // slab_w + 1) * slab_w)
            parts.append(slab_refs[lo // slab_w][:, lo % slab_w:lo % slab_w + (end - lo)])
            lo = end
        return parts[0] if len(parts) == 1 else jnp.concatenate(parts, axis=1)

    @pl.when(pl.program_id(1) == 0)
    def _():
        s_st[...] = s0_ref[...]
        m_st[...] = m0_ref[...]
        t_st[...] = t0_ref[...]

    row_c = lax.broadcasted_iota(jnp.int32, (c, 1), 0)
    lane = lax.broadcasted_iota(jnp.int32, (1, LANES), 1)
    y_ref[:, :y_off] = jnp.zeros((c, y_off), y_ref.dtype)

    gates = jnp.zeros((c, 2 * LANES), F32) + bg_ref[...]
    for j in range(n_blk):
        sl = slice(j * QKV_TILE, (j + 1) * QKV_TILE)
        xf = slab_cols(xb_refs, j * QKV_TILE, (j + 1) * QKV_TILE)
        if pad_rows:
            xf = jnp.where(row_c >= pad_rows, xf, 0.0)
        xc = _causal_conv_silu(xf, t_st[:, sl], cw_ref[:, sl], cb_ref[:, sl])
        t_st[:, sl] = xf[c - SUBLANES:]
        xcb = xc.astype(BF16)
        qg = jnp.dot(xcb, wq_ref[j], preferred_element_type=F32)
        vg = jnp.dot(xf.astype(BF16), wv_ref[j], preferred_element_type=F32)
        q_scr[:, sl] = qg[:, :QKV_TILE] * scale
        kt_scr[sl, :] = _dot_nt(wkt_ref[j], xcb)
        v_scr[:, sl] = vg[:, :QKV_TILE].astype(BF16)
        xc_scr[:, sl] = xc
        gates = gates + qg[:, QKV_TILE:] + vg[:, QKV_TILE:]

    ig = gates[:, :LANES]
    lf = _log_sigmoid(gates[:, LANES:])
    if pad_rows:
        ig = jnp.where(row_c >= pad_rows, ig, -jnp.inf)
        lf = jnp.where(row_c >= pad_rows, lf, 0.0)
    causal = _causal_masks(c)
    bcum = _dot_f32(causal.astype(F32), lf)
    a = ig - bcum
    m_old = m_st[...]
    big_m = jnp.maximum(_cummax_rows(a), m_old)
    w_prev = jnp.exp(m_old - big_m)
    e_inv = jnp.exp(-(bcum + big_m))
    m_last = big_m[c - 1:c]
    decay = jnp.exp(m_old - m_last)
    e_col = jnp.exp(a - m_last)
    m_st[...] = bcum[c - 1:c] + m_last
    a2_t = (a * LOG2E).T
    m2 = big_m * LOG2E
    e_t = e_col.T

    for h in range(n_heads):
        hs = slice(h * dh, (h + 1) * dh)
        wt = jnp.exp2(jnp.where(causal, a2_t[h:h + 1, :] - m2[:, h:h + 1], -jnp.inf))
        kt = kt_scr[hs, :]
        qh = q_scr[:, hs]
        s = _dot(qh, kt) * wt
        lhs = jnp.concatenate([s.astype(BF16), (qh * w_prev[:, h:h + 1]).astype(BF16)], axis=1)
        ones_h = jnp.broadcast_to(jnp.where(lane == h, 1.0, 0.0).astype(BF16), (c, LANES))
        vext = jnp.concatenate([v_scr[:, hs], ones_h], axis=1)
        st = s_st[h]
        nd = jnp.dot(lhs, jnp.concatenate([vext, st.astype(BF16)], axis=0),
                     preferred_element_type=F32)
        r_blk = 1.0 / jnp.maximum(jnp.abs(nd[:, dh:]), e_inv)
        hh = nd[:, :dh] * r_blk[:, h:h + 1]
        wkt = (kt * e_t[h:h + 1, :]).astype(BF16)
        s_st[h] = decay[:, h:h + 1] * st + jnp.dot(wkt, vext, preferred_element_type=F32)
        mu = jnp.mean(hh, axis=1, keepdims=True)
        hc = hh - mu
        var = jnp.mean(hc * hc, axis=1, keepdims=True)
        hn = hc * lax.rsqrt(var + HEAD_NORM_EPS) * nw_ref[:, hs]
        out = (hn + sk_ref[:, hs] * xc_scr[:, hs]) * _silu_of_half(slab_cols(zb_refs, h * dh, (h + 1) * dh))
        y_ref[:, y_off + h * dh:y_off + (h + 1) * dh] = out.astype(y_ref.dtype)


def _state_specs(shapes, emit_state):
    st_in = [pl.BlockSpec((None,) + s, lambda b, i, nd=len(s): (0,) * (nd + 1),
                          pipeline_mode=pl.Buffered(1)) for s in shapes]
    if emit_state:
        st_out = [pl.BlockSpec((None,) + s, lambda b, i, nd=len(s): (b,) + (0,) * nd) for s in shapes]
        return st_in, st_out, []
    return st_in, [], [pltpu.VMEM(s, F32) for s in shapes]


def _mlstm_call(p, col_x, col_z, slab_w, out_width, wts, state0, bsz, n_chunks, pad_rows, emit_state):
    c = CHUNK
    width = wts["cw"].shape[1]
    nh = MLSTM_HEADS
    dh = width // nh
    n_slabs = width // slab_w
    row = lambda b, i: b * n_chunks + i
    slab_spec = lambda col: pl.BlockSpec((c, slab_w), lambda b, i: (row(b, i), col))
    shapes = [(nh, dh, dh + LANES), (1, LANES), (SUBLANES, width)]
    st_in, st_out, st_scr = _state_specs(shapes, emit_state)
    names = ["cw", "cb", "wq", "wkt", "wv", "bg", "nw", "sk"]
    work = [pltpu.VMEM((c, width), F32), pltpu.VMEM((width, c), F32),
            pltpu.VMEM((c, width), BF16), pltpu.VMEM((c, width), F32)]
    return pl.pallas_call(
        functools.partial(_mlstm_kernel, n_in_slabs=n_slabs, pad_rows=pad_rows, emit_state=emit_state),
        grid=(bsz, n_chunks),
        in_specs=[slab_spec(col_x + s) for s in range(n_slabs)]
        + [slab_spec(col_z + s) for s in range(n_slabs)]
        + [_const_spec(wts[n].shape) for n in names] + st_in,
        out_specs=[pl.BlockSpec((c, out_width), lambda b, i: (row(b, i), 0))] + st_out,
        out_shape=[jax.ShapeDtypeStruct((bsz * n_chunks * c, out_width), BF16)]
        + ([jax.ShapeDtypeStruct((bsz,) + s, F32) for s in shapes] if emit_state else []),
        scratch_shapes=work + st_scr,
        compiler_params=_params(2),
        name="mlstm_mixer",
    )(*([p] * (2 * n_slabs)), *[wts[n] for n in names], *state0)


def _block_diag_tiles(w, tile):
    nb, b, _ = w.shape
    per = tile // b
    w = w.reshape(nb // per, per, b, b)
    eye = jnp.eye(per, dtype=w.dtype)
    return jnp.einsum("jnio,nN->jniNo", w, eye).reshape(nb // per, tile, tile)


def _mlstm_weights(conv_w, conv_b, wq, wk, wv, w_gate, b_gate, norm_w, skip):
    width = conv_w.shape[1]
    nh = MLSTM_HEADS
    n_blk = width // QKV_TILE

    def gate_cols(w_hw, w_g):
        g = jnp.einsum("nio,nog->nig", w_hw, w_g.reshape(-1, QKV_BLOCK, 2 * nh)).reshape(width, 2 * nh)
        out = jnp.zeros((width, 2 * LANES), F32)
        out = out.at[:, :nh].set(g[:, :nh]).at[:, LANES:LANES + nh].set(g[:, nh:])
        return out.reshape(n_blk, QKV_TILE, 2 * LANES)

    g_qk = gate_cols(wq, w_gate[:width]) + gate_cols(wk, w_gate[width:2 * width])
    g_v = gate_cols(wv, w_gate[2 * width:])
    wq_t = jnp.concatenate([_block_diag_tiles(wq, QKV_TILE), g_qk], axis=2).astype(BF16)
    wv_t = jnp.concatenate([_block_diag_tiles(wv, QKV_TILE), g_v], axis=2).astype(BF16)
    wk_t = _block_diag_tiles(wk, QKV_TILE).transpose(0, 2, 1).astype(BF16)
    bg = jnp.zeros((1, 2 * LANES), F32)
    bg = bg.at[0, :nh].set(b_gate[:nh]).at[0, LANES:LANES + nh].set(b_gate[nh:])
    return dict(cw=0.5 * conv_w, cb=0.5 * conv_b.reshape(1, width), wq=wq_t, wkt=wk_t, wv=wv_t, bg=bg,
                nw=norm_w.reshape(1, width), sk=skip.reshape(1, width))


def _ssd_kernel(*refs, pad_rows, emit_state):
    (z_ref, xs_ref, bm_ref, cm_ref, dt_ref, cwx_ref, cbx_ref, cwb_ref, cbb_ref, cwc_ref, cbc_ref,
     dtb_ref, a_ref, d_ref, gn_ref, s0_ref, tx0_ref, tb0_ref, tc0_ref, y_ref) = refs[:20]
    if emit_state:
        s_st, tx_st, tb_st, tc_st, xs_scr, bm_scr, cm_scr, y_scr = refs[20:]
    else:
        xs_scr, bm_scr, cm_scr, y_scr, s_st, tx_st, tb_st, tc_st = refs[20:]
    c, inner = xs_ref.shape
    n_pairs = s_st.shape[0]
    pairs_per_group = n_pairs // SSD_GROUPS
    gw = inner // SSD_GROUPS

    @pl.when(pl.program_id(1) == 0)
    def _():
        s_st[...] = s0_ref[...]
        tx_st[...] = tx0_ref[...]
        tb_st[...] = tb0_ref[...]
        tc_st[...] = tc0_ref[...]

    row_c = lax.broadcasted_iota(jnp.int32, (c, 1), 0)

    def conv_into(dst_scr, src_ref, tail_st, w_ref, b_ref):
        for j in range(src_ref.shape[1] // CONV_BLOCK):
            sl = slice(j * CONV_BLOCK, (j + 1) * CONV_BLOCK)
            x = src_ref[:, sl]
            if pad_rows:
                x = jnp.where(row_c >= pad_rows, x, 0.0)
            dst_scr[:, sl] = _causal_conv_silu(x, tail_st[:, sl], w_ref[:, sl], b_ref[:, sl])
            tail_st[:, sl] = x[c - SUBLANES:]

    conv_into(xs_scr, xs_ref, tx_st, cwx_ref, cbx_ref)
    conv_into(bm_scr, bm_ref, tb_st, cwb_ref, cbb_ref)
    conv_into(cm_scr, cm_ref, tc_st, cwc_ref, cbc_ref)

    dt = jax.nn.softplus(dt_ref[...] + dtb_ref[...])
    if pad_rows:
        dt = jnp.where(row_c >= pad_rows, dt, 0.0)
    causal = _causal_masks(c)
    cum = _dot_f32(causal.astype(F32), dt * a_ref[...])
    cum_t = cum.T
    dt_t = dt.T
    dec_t = jnp.exp(cum_t[:, c - 1:c] - cum_t) * dt_t
    src2_t = (cum_t - jnp.log(dt_t)) * LOG2E
    cum2 = cum * LOG2E
    e_last = jnp.exp(cum[c - 1:c, :])
    lo = lax.broadcasted_iota(jnp.int32, (1, LANES), 1) < SSD_HEAD_DIM

    for g in range(SSD_GROUPS):
        gs = slice(g * SSD_STATE, (g + 1) * SSD_STATE)
        cg = cm_scr[:, gs]
        bg_t = bm_scr[:, gs].T
        cb = _dot(cg, bg_t)
        for pr in range(pairs_per_group):
            pidx = g * pairs_per_group + pr
            ps = slice(pidx * LANES, (pidx + 1) * LANES)
            xp = xs_scr[:, ps]
            xpb = xp.astype(BF16)
            st = s_st[pidx]
            rhs = jnp.concatenate([xpb, st.astype(BF16)], axis=0)
            ys, ups = [], []
            for h in (2 * pidx, 2 * pidx + 1):
                bc = jnp.broadcast_to(cum2[:, h:h + 1], (c, c))
                seg = jnp.exp2(jnp.where(causal, bc - src2_t[h:h + 1, :], -jnp.inf))
                lhs = jnp.concatenate([(cb * seg).astype(BF16), (cg * jnp.exp2(bc)).astype(BF16)],
                                      axis=1)
                ys.append(jnp.dot(lhs, rhs, preferred_element_type=F32))
                ups.append(jnp.dot((bg_t * dec_t[h:h + 1, :]).astype(BF16), xpb,
                                   preferred_element_type=F32))
            h0 = 2 * pidx
            y_scr[:, ps] = jnp.where(lo, ys[0], ys[1]) + d_ref[:, ps] * xp
            el = jnp.where(lo, e_last[:, h0:h0 + 1], e_last[:, h0 + 1:h0 + 2])
            s_st[pidx] = el * st + jnp.where(lo, ups[0], ups[1])

    for g in range(SSD_GROUPS):
        gs = slice(g * gw, (g + 1) * gw)
        yg = y_scr[:, gs] * _silu_of_half(z_ref[:, gs])
        yg = yg * lax.rsqrt(jnp.mean(yg * yg, axis=1, keepdims=True) + NORM_EPS)
        y_ref[:, gs] = (yg * gn_ref[:, gs]).astype(y_ref.dtype)


def _ssd_call(p, pdt, wts, state0, bsz, n_chunks, pad_rows, emit_state):
    c = CHUNK
    inner = wts["d"].shape[1]
    gn = wts["cwb"].shape[1]
    n_pairs = inner // LANES
    row = lambda b, i: b * n_chunks + i
    shapes = [(n_pairs, SSD_STATE, LANES), (SUBLANES, inner), (SUBLANES, gn), (SUBLANES, gn)]
    st_in, st_out, st_scr = _state_specs(shapes, emit_state)
    names = ["cwx", "cbx", "cwb", "cbb", "cwc", "cbc", "dtb", "a", "d", "gn"]
    bcol = 2 * inner // gn
    work = [pltpu.VMEM((c, inner), F32), pltpu.VMEM((c, gn), F32), pltpu.VMEM((c, gn), F32),
            pltpu.VMEM((c, inner), F32)]
    return pl.pallas_call(
        functools.partial(_ssd_kernel, pad_rows=pad_rows, emit_state=emit_state),
        grid=(bsz, n_chunks),
        in_specs=[pl.BlockSpec((c, inner), lambda b, i: (row(b, i), 0)),
                  pl.BlockSpec((c, inner), lambda b, i: (row(b, i), 1)),
                  pl.BlockSpec((c, gn), lambda b, i: (row(b, i), bcol)),
                  pl.BlockSpec((c, gn), lambda b, i: (row(b, i), bcol + 1)),
                  pl.BlockSpec((c, LANES), lambda b, i: (row(b, i), 0))]
        + [_const_spec(wts[n].shape) for n in names] + st_in,
        out_specs=[pl.BlockSpec((c, inner), lambda b, i: (row(b, i), 0))] + st_out,
        out_shape=[jax.ShapeDtypeStruct((bsz * n_chunks * c, inner), BF16)]
        + ([jax.ShapeDtypeStruct((bsz,) + s, F32) for s in shapes] if emit_state else []),
        scratch_shapes=work + st_scr,
        compiler_params=_params(2),
        name="ssd_mixer",
    )(p, p, p, p, pdt, *[wts[n] for n in names], *state0)


def _ssd_weights(conv_w, conv_b, dt_bias, a_log, d, gnorm):
    n_heads = dt_bias.shape[0]
    inner = gnorm.shape[0]
    gn = (conv_w.shape[1] - inner) // 2
    pad = lambda v: jnp.zeros((1, LANES), F32).at[0, :n_heads].set(v)
    cb = 0.5 * conv_b.reshape(1, -1)
    conv_w = 0.5 * conv_w
    return dict(cwx=conv_w[:, :inner], cbx=cb[:, :inner],
                cwb=conv_w[:, inner:inner + gn], cbb=cb[:, inner:inner + gn],
                cwc=conv_w[:, inner + gn:], cbc=cb[:, inner + gn:],
                dtb=pad(dt_bias), a=pad(-jnp.exp(a_log)),
                d=jnp.repeat(d, inner // n_heads).reshape(1, inner), gn=gnorm.reshape(1, inner))


def kernel(x, meta_tokens, ab_norm, ab_w_in, s5_lambda_re, s5_lambda_im, s5_log_dt, s5_b_re, s5_b_im, s5_c_re, s5_c_im, s5_d, s5_glu_w, s5_glu_b, ml_conv_w, ml_conv_b, ml_wq, ml_wk, ml_wv, ml_w_gate, ml_b_gate, ml_norm, ml_skip, ab_w_out, ssd_norm, ssd_w_in, ssd_conv_w, ssd_conv_b, ssd_dt_bias, ssd_a_log, ssd_d, ssd_gnorm, ssd_w_out, final_norm):
    bsz, seq, d_model = x.shape
    assert seq % CHUNK == 0 and N_META <= CHUNK
    n_chunks = seq // CHUNK
    s5_w = s5_d.shape[1]
    ml_w = ml_conv_w.shape[2]
    ssd_inner = ssd_gnorm.shape[1]
    ssd_heads = ssd_dt_bias.shape[1]
    ssd_main = ssd_w_in.shape[2] - ssd_heads
    pad_rows = CHUNK - N_META

    half = lambda n: jnp.full((n,), 0.5, F32)
    one = lambda n: jnp.ones((n,), F32)
    w_in0 = (ab_w_in[0] * jnp.concatenate([one(s5_w), half(s5_w), one(ml_w), half(ml_w)])).astype(BF16)
    slab_w = s5_w
    w_out0 = ab_w_out[0].astype(BF16)
    s5_wts = _s5_weights(s5_lambda_re[0], s5_lambda_im[0], s5_log_dt[0], s5_b_re[0], s5_b_im[0],
                         s5_c_re[0], s5_c_im[0], s5_d[0], s5_glu_w[0], s5_glu_b[0])
    ml_wts = _mlstm_weights(ml_conv_w[0], ml_conv_b[0], ml_wq[0], ml_wk[0], ml_wv[0],
                            ml_w_gate[0], ml_b_gate[0], ml_norm[0], ml_skip[0])
    w_in1_t = jnp.swapaxes(ssd_w_in[0], 0, 1)
    w_in1_main = (w_in1_t[:ssd_main]
                  * jnp.concatenate([half(ssd_inner), one(ssd_main - ssd_inner)])[:, None]).astype(BF16)
    w_in1_dt = jnp.zeros((LANES, d_model), F32).at[:ssd_heads].set(w_in1_t[ssd_main:]).astype(BF16)
    w_out1 = ssd_w_out[0].astype(BF16)
    ssd_wts = _ssd_weights(ssd_conv_w[0], ssd_conv_b[0], ssd_dt_bias[0], ssd_a_log[0], ssd_d[0],
                           ssd_gnorm[0])

    n_tiles = s5_wts["apr"].shape[0]
    dh = ml_w // MLSTM_HEADS
    zeros = lambda *s: jnp.zeros(s, F32)
    s5_zero = (zeros(1, n_tiles, 1, LANES), zeros(1, n_tiles, 1, LANES))
    ml_zero = (zeros(1, MLSTM_HEADS, dh, dh + LANES), zeros(1, 1, LANES), zeros(1, SUBLANES, ml_w))
    gn = (ssd_main - 2 * ssd_inner) // 2
    ssd_zero = (zeros(1, ssd_inner // LANES, SSD_STATE, LANES), zeros(1, SUBLANES, ssd_inner),
                zeros(1, SUBLANES, gn), zeros(1, SUBLANES, gn))

    def layer0(h, b, nc, pad, s5_state, ml_state, emit_state):
        p = _norm_proj(h, ab_norm[0], w_in0, name="ab_in_proj")[0]
        y, *ml_out = _mlstm_call(p, 2 * s5_w // slab_w, (2 * s5_w + ml_w) // slab_w, slab_w,
                                 s5_w + ml_w, ml_wts, ml_state, b, nc, pad, emit_state)
        y, sr, si = _s5_call(p, 0, 1, y, 0, s5_wts, s5_state, b, nc, pad)
        h = _res_proj(y, w_out0, h, name="ab_out_proj")
        return h, (sr, si), tuple(ml_out)

    def layer1_mixer(h, b, nc, pad, ssd_state, emit_state):
        p, pdt = _norm_proj(h, ssd_norm[0], w_in1_main, w_in1_dt, w_transposed=True,
                            name="ssd_in_proj")
        y, *st = _ssd_call(p, pdt, ssd_wts, ssd_state, b, nc, pad, emit_state)
        return y, tuple(st)

    hm = jnp.concatenate([zeros(pad_rows, d_model), meta_tokens.astype(F32)], axis=0)
    hm, s5_state, ml_state = layer0(hm, 1, 1, pad_rows, s5_zero, ml_zero, True)
    _, ssd_state = layer1_mixer(hm, 1, 1, pad_rows, ssd_zero, True)

    h = x.reshape(bsz * seq, d_model)
    h, _, _ = layer0(h, bsz, n_chunks, 0, s5_state, ml_state, False)
    y, _ = layer1_mixer(h, bsz, n_chunks, 0, ssd_state, False)
    out = _res_proj_norm(y, w_out1, h, final_norm, name="ssd_out_proj_norm")
    return out.reshape(bsz, seq, d_model)
```

```python
import functools

import jax
import jax.numpy as jnp
from jax import lax
from jax.experimental import pallas as pl
from jax.experimental.pallas import tpu as pltpu

F32 = jnp.float32
BF16 = jnp.bfloat16

N_META = 16
CHUNK = 128
NORM_EPS = 1e-6
HEAD_NORM_EPS = 1e-5
LANES = 128
SUBLANES = 8
VMEM_LIMIT = 56 * 1024 * 1024

S5_GROUP_SIZE = 16
S5_STATE = 64
S5_OCT = 8
MLSTM_HEADS = 8
QKV_BLOCK = 4
QKV_TILE = 256
SSD_HEAD_DIM = 64
SSD_STATE = 128
SSD_GROUPS = 8
CONV_BLOCK = 512
LOG2E = 1.4426950408889634
NORM_PROJ_TN = 1024
SUBCHUNKS = 2


def _params(n_grid):
    return pltpu.CompilerParams(dimension_semantics=("arbitrary",) * n_grid,
                                vmem_limit_bytes=VMEM_LIMIT)


def _const_spec(shape):
    nd = len(shape)
    return pl.BlockSpec(shape, lambda *_: (0,) * nd, pipeline_mode=pl.Buffered(1))


def _dot(a, b):
    return jnp.dot(a.astype(BF16), b.astype(BF16), preferred_element_type=F32)


def _dot_nt(a, b):
    return lax.dot_general(a.astype(BF16), b.astype(BF16), (((1,), (1,)), ((), ())),
                           preferred_element_type=F32)


def _dot_f32(a, b):
    return jnp.dot(a, b, preferred_element_type=F32, precision=lax.Precision.HIGHEST)


def _sigmoid(x):
    return 0.5 * (1.0 + jnp.tanh(0.5 * x))


def _silu_of_half(h):
    return h + h * jnp.tanh(h)


def _shift_rows(x, d, prev8):
    rolled = pltpu.roll(x, d, axis=0)
    row8 = lax.broadcasted_iota(jnp.int32, (SUBLANES, x.shape[1]), 0)
    head = jnp.where(row8 < d, pltpu.roll(prev8, d, axis=0), rolled[:SUBLANES])
    if x.shape[0] == SUBLANES:
        return head
    return jnp.concatenate([head, rolled[SUBLANES:]], axis=0)


def _causal_conv_silu(x, tail8, w_half, b_half):
    assert w_half.shape[0] == 4
    w0, w1, w2, w3 = (w_half[i:i + 1, :] for i in range(4))
    x2 = _shift_rows(x, 2, tail8)
    u = x * w2 + x2 * w0
    u_tail = tail8 * w2 + pltpu.roll(tail8, 2, axis=0) * w0
    h = x * w3 + x2 * w1 + b_half + _shift_rows(u, 1, u_tail)
    return h + h * jnp.tanh(h)


def _causal_masks(c):
    r_i = lax.broadcasted_iota(jnp.int32, (c, c), 0)
    c_i = lax.broadcasted_iota(jnp.int32, (c, c), 1)
    return r_i >= c_i


def _for_subchunks(chunk_fn, refs, row_idx, n_sub):
    first_step = pl.program_id(1) == 0
    if n_sub == 1:
        chunk_fn(*refs, is_first=first_step)
        return

    def step(s, carry):
        rows = pl.ds(pl.multiple_of(s * CHUNK, CHUNK), CHUNK)
        sub = [r.at[rows, :] if k in row_idx else r for k, r in enumerate(refs)]
        chunk_fn(*sub, is_first=jnp.logical_and(first_step, s == 0))
        return carry

    lax.fori_loop(0, n_sub, step, 0)


def _step_layout(n_chunks):
    n_sub = SUBCHUNKS if n_chunks % SUBCHUNKS == 0 else 1
    steps = n_chunks // n_sub
    return n_sub, steps, lambda b, i: b * steps + i


def _norm_proj_kernel(a_ref, g_ref, w_ref, *rest, has_extra, w_transposed):
    if has_extra:
        wx_ref, o_ref, ox_ref, a_scr = rest
    else:
        o_ref, a_scr = rest
    tm = a_ref.shape[0]
    slab = min(tm, CHUNK)

    def mm(lhs, rhs):
        if w_transposed:
            return lax.dot_general(lhs, rhs, (((1,), (1,)), ((), ())), preferred_element_type=F32)
        return jnp.dot(lhs, rhs, preferred_element_type=F32)

    @pl.when(pl.program_id(1) == 0)
    def _():
        def norm_slab(i, carry):
            rows = pl.ds(pl.multiple_of(i * slab, slab), slab)
            x = a_ref[rows, :]
            y = x * lax.rsqrt(jnp.mean(x * x, axis=-1, keepdims=True) + NORM_EPS)
            a_scr[rows, :] = (y * g_ref[...]).astype(BF16)
            return carry

        lax.fori_loop(0, tm // slab, norm_slab, 0)
        if has_extra:
            ox_ref[...] = mm(a_scr[...], wx_ref[...])

    o_ref[...] = mm(a_scr[...], w_ref[...])


def _norm_proj(a, g, w, w_extra=None, w_transposed=False, name="norm_proj"):
    m, k = a.shape
    n = w.shape[0] if w_transposed else w.shape[1]
    tm = min(m, 1024)
    tn = NORM_PROJ_TN
    w_spec = (pl.BlockSpec((tn, k), lambda i, j: (j, 0)) if w_transposed
              else pl.BlockSpec((k, tn), lambda i, j: (0, j)))
    in_specs = [pl.BlockSpec((tm, k), lambda i, j: (i, 0)), _const_spec((1, k)), w_spec]
    out_specs = [pl.BlockSpec((tm, tn), lambda i, j: (i, j))]
    out_shape = [jax.ShapeDtypeStruct((m, n), F32)]
    args = [a, g.reshape(1, k), w]
    if w_extra is not None:
        nx = w_extra.shape[0] if w_transposed else w_extra.shape[1]
        in_specs.append(_const_spec(w_extra.shape))
        out_specs.append(pl.BlockSpec((tm, nx), lambda i, j: (i, 0)))
        out_shape.append(jax.ShapeDtypeStruct((m, nx), F32))
        args.append(w_extra)
    return pl.pallas_call(
        functools.partial(_norm_proj_kernel, has_extra=w_extra is not None,
                          w_transposed=w_transposed),
        grid=(m // tm, n // tn),
        in_specs=in_specs, out_specs=out_specs, out_shape=out_shape,
        scratch_shapes=[pltpu.VMEM((tm, k), BF16)],
        compiler_params=_params(2),
        name=name,
    )(*args)


def _res_proj_kernel(a_ref, w_ref, r_ref, o_ref):
    o_ref[...] = jnp.dot(a_ref[...], w_ref[...], preferred_element_type=F32) + r_ref[...]


def _res_proj(a, w, res, name="res_proj"):
    m, k = a.shape
    n = w.shape[1]
    tm = min(m, 1024)
    tn = 1024
    return pl.pallas_call(
        _res_proj_kernel,
        grid=(m // tm, n // tn),
        in_specs=[pl.BlockSpec((tm, k), lambda i, j: (i, 0)),
                  pl.BlockSpec((k, tn), lambda i, j: (0, j)),
                  pl.BlockSpec((tm, tn), lambda i, j: (i, j))],
        out_specs=pl.BlockSpec((tm, tn), lambda i, j: (i, j)),
        out_shape=jax.ShapeDtypeStruct((m, n), F32),
        compiler_params=_params(2),
        name=name,
    )(a, w, res)


def _res_proj_norm_kernel(a_ref, w_ref, r_ref, g_ref, o_ref):
    tm = a_ref.shape[0]
    slab = min(tm, CHUNK)
    for s in range(tm // slab):
        rows = slice(s * slab, (s + 1) * slab)
        x = jnp.dot(a_ref[rows, :], w_ref[...], preferred_element_type=F32) + r_ref[rows, :]
        y = x * lax.rsqrt(jnp.mean(x * x, axis=-1, keepdims=True) + NORM_EPS)
        o_ref[rows, :] = y * g_ref[...]


def _res_proj_norm(a, w, res, g, name="res_proj_norm"):
    m, k = a.shape
    n = w.shape[1]
    tm = min(m, 512)
    return pl.pallas_call(
        _res_proj_norm_kernel,
        grid=(m // tm,),
        in_specs=[pl.BlockSpec((tm, k), lambda i: (i, 0)), _const_spec((k, n)),
                  pl.BlockSpec((tm, n), lambda i: (i, 0)), _const_spec((1, n))],
        out_specs=pl.BlockSpec((tm, n), lambda i: (i, 0)),
        out_shape=jax.ShapeDtypeStruct((m, n), F32),
        compiler_params=_params(1),
        name=name,
    )(a, w, res, g.reshape(1, n))


def _s5_kernel(*refs, pad_rows, n_sub):
    _for_subchunks(functools.partial(_s5_chunk, pad_rows=pad_rows), refs, (0, 1, 12), n_sub)


def _s5_chunk(u_ref, z_ref, wb_ref, wc_ref, apr_ref, api_ref, d_ref, gw_ref, gb_ref,
              sr0_ref, si0_ref, yin_ref, y_ref, sr_ref, si_ref, out_scr, *, pad_rows, is_first):
    del yin_ref
    c = u_ref.shape[0]
    seg = c // SUBLANES
    n_tiles = apr_ref.shape[0]
    tiles_per_oct = n_tiles // S5_OCT
    half = tiles_per_oct * LANES

    @pl.when(is_first)
    def _():
        sr_ref[...] = sr0_ref[...]
        si_ref[...] = si0_ref[...]

    n_slabs = out_scr.shape[0]

    def load_permuted(ref):
        for k in range(n_slabs):
            out_scr[k] = ref[:, k * LANES:(k + 1) * LANES]
        return jnp.concatenate(
            [jnp.concatenate([out_scr[k, pl.ds(i, SUBLANES, stride=seg), :] for i in range(seg)], axis=0)
             for k in range(n_slabs)], axis=1)

    u = load_permuted(u_ref)
    if pad_rows:
        row = lax.broadcasted_iota(jnp.int32, (c, 1), 0)
        t_of_row = (row & (SUBLANES - 1)) * seg + (row >> 3)
        u = jnp.where(t_of_row >= pad_rows, u, 0.0)

    zeros8 = jnp.zeros((SUBLANES, LANES), F32)

    def cmul_add(xr, xi, pr, pi, yr, yi):
        return xr + pr * yr - pi * yi, xi + pr * yi + pi * yr

    def scan_tile(j, bur, bui):
        blk = lambda i: slice(i * SUBLANES, (i + 1) * SUBLANES)
        a1r, a1i = apr_ref[j, 0:1, :], api_ref[j, 0:1, :]
        br, bi = bur[blk(0)], bui[blk(0)]
        loc = [(br, bi)]
        for i in range(1, seg):
            br, bi = cmul_add(bur[blk(i)], bui[blk(i)], a1r, a1i, br, bi)
            loc.append((br, bi))
        fr, fi = br, bi
        for d in (1, 2, 4):
            p = seg * d - 1
            fr, fi = cmul_add(fr, fi, apr_ref[j, p:p + 1, :], api_ref[j, p:p + 1, :],
                              _shift_rows(fr, d, zeros8), _shift_rows(fi, d, zeros8))
        cr, ci = sr_ref[j], si_ref[j]
        fr, fi = cmul_add(fr, fi, apr_ref[j, pl.ds(seg - 1, SUBLANES, stride=seg), :],
                          api_ref[j, pl.ds(seg - 1, SUBLANES, stride=seg), :], cr, ci)
        gr = _shift_rows(fr, 1, jnp.broadcast_to(cr, (SUBLANES, LANES)))
        gi = _shift_rows(fi, 1, jnp.broadcast_to(ci, (SUBLANES, LANES)))
        xs = [cmul_add(loc[i][0], loc[i][1], apr_ref[j, i:i + 1, :], api_ref[j, i:i + 1, :], gr, gi)
              for i in range(seg)]
        sr_ref[j] = fr[SUBLANES - 1:SUBLANES]
        si_ref[j] = fi[SUBLANES - 1:SUBLANES]
        return (jnp.concatenate([x[0] for x in xs], axis=0),
                jnp.concatenate([x[1] for x in xs], axis=0))

    ys = []
    for k in range(S5_OCT):
        bu = _dot(u[:, k * LANES:(k + 1) * LANES], wb_ref[k])
        xrs, xis = [], []
        for q in range(tiles_per_oct):
            xr, xi = scan_tile(k * tiles_per_oct + q, bu[:, q * LANES:(q + 1) * LANES],
                               bu[:, half + q * LANES:half + (q + 1) * LANES])
            xrs.append(xr)
            xis.append(xi)
        ys.append(_dot(jnp.concatenate(xrs + xis, axis=1), wc_ref[k]))
    y = jnp.concatenate(ys, axis=1) + d_ref[...] * u
    g = jax.nn.gelu(y)
    gate = _sigmoid(_dot(g, gw_ref[...]) + gb_ref[...])
    res = g * gate * _silu_of_half(load_permuted(z_ref))
    for k in range(n_slabs):
        for i in range(seg):
            out_scr[k, pl.ds(i, SUBLANES, stride=seg), :] = res[i * SUBLANES:(i + 1) * SUBLANES,
                                                                k * LANES:(k + 1) * LANES]
        y_ref[:, k * LANES:(k + 1) * LANES] = out_scr[k].astype(y_ref.dtype)


def _s5_call(p, col_u, col_z, y_buf, col_y, wts, state0, bsz, n_chunks, pad_rows):
    c = CHUNK
    width = wts["d"].shape[1]
    n_tiles = wts["apr"].shape[0]
    n_sub, steps, row = _step_layout(n_chunks)
    st_spec_in = pl.BlockSpec((None, n_tiles, 1, LANES), lambda b, i: (0, 0, 0, 0),
                              pipeline_mode=pl.Buffered(1))
    st_spec_out = pl.BlockSpec((None, n_tiles, 1, LANES), lambda b, i: (b, 0, 0, 0))
    st_shape = jax.ShapeDtypeStruct((bsz, n_tiles, 1, LANES), F32)
    names = ["wb", "wc", "apr", "api", "d", "gw", "gb"]
    return pl.pallas_call(
        functools.partial(_s5_kernel, pad_rows=pad_rows, n_sub=n_sub),
        grid=(bsz, steps),
        in_specs=[pl.BlockSpec((c * n_sub, width), lambda b, i: (row(b, i), col_u)),
                  pl.BlockSpec((c * n_sub, width), lambda b, i: (row(b, i), col_z))]
        + [_const_spec(wts[n].shape) for n in names]
        + [st_spec_in, st_spec_in, pl.BlockSpec(memory_space=pl.ANY)],
        out_specs=[pl.BlockSpec((c * n_sub, width), lambda b, i: (row(b, i), col_y)),
                   st_spec_out, st_spec_out],
        out_shape=[jax.ShapeDtypeStruct(y_buf.shape, y_buf.dtype), st_shape, st_shape],
        scratch_shapes=[pltpu.VMEM((width // LANES, c, LANES), F32)],
        input_output_aliases={len(names) + 4: 0},
        compiler_params=_params(2),
        name="s5_mixer",
    )(p, p, *[wts[n] for n in names], state0[0], state0[1], y_buf)


def _s5_weights(lam_re, lam_im, log_dt, b_re, b_im, c_re, c_im, d, glu_w, glu_b):
    g_n, p_n = lam_re.shape
    dt = jnp.exp(log_dt)[:, None]
    mag = jnp.exp(lam_re * dt)
    ar, ai = mag * jnp.cos(lam_im * dt), mag * jnp.sin(lam_im * dt)
    den = lam_re * lam_re + lam_im * lam_im
    qr = ((ar - 1.0) * lam_re + ai * lam_im) / den
    qi = (ai * lam_re - (ar - 1.0) * lam_im) / den
    bbr = qr[..., None] * b_re - qi[..., None] * b_im
    bbi = qr[..., None] * b_im + qi[..., None] * b_re
    n_oct = g_n // S5_OCT
    eye = jnp.eye(S5_OCT, dtype=F32)

    def in_proj(w):
        w = w.reshape(n_oct, S5_OCT, p_n, S5_GROUP_SIZE)
        return jnp.einsum("kgph,gG->kghGp", w, eye).reshape(n_oct, S5_OCT * S5_GROUP_SIZE, S5_OCT * p_n)

    def out_proj(w):
        w = w.reshape(n_oct, S5_OCT, S5_GROUP_SIZE, p_n)
        return jnp.einsum("kghp,gG->kgpGh", w, eye).reshape(n_oct, S5_OCT * p_n, S5_OCT * S5_GROUP_SIZE)

    wb = jnp.concatenate([in_proj(bbr), in_proj(bbi)], axis=2).astype(BF16)
    wc = jnp.concatenate([out_proj(c_re), out_proj(-c_im)], axis=1).astype(BF16)
    pr, pi = ar.reshape(1, -1), ai.reshape(1, -1)
    while pr.shape[0] < CHUNK:
        lr_, li_ = pr[-1:], pi[-1:]
        pr, pi = (jnp.concatenate([pr, pr * lr_ - pi * li_], axis=0),
                  jnp.concatenate([pi, pr * li_ + pi * lr_], axis=0))
    n_tiles = g_n * p_n // LANES
    tile = lambda a: a.reshape(CHUNK, n_tiles, LANES).transpose(1, 0, 2)
    width = d.shape[0]
    return dict(wb=wb, wc=wc, apr=tile(pr), api=tile(pi), d=d.reshape(1, width),
                gw=glu_w.astype(BF16), gb=glu_b.reshape(1, width))


def _log_sigmoid(x):
    return jnp.minimum(x, 0.0) - jnp.log1p(jnp.exp(-jnp.abs(x)))


def _cummax_rows(x):
    c = x.shape[0]
    neg8 = jnp.full((SUBLANES, x.shape[1]), -jnp.inf, F32)
    d = 1
    while d < c:
        if d < SUBLANES:
            sh = _shift_rows(x, d, neg8)
        else:
            sh = jnp.concatenate([jnp.full((d, x.shape[1]), -jnp.inf, F32), x[:c - d]], axis=0)
        x = jnp.maximum(x, sh)
        d *= 2
    return x


def _mlstm_kernel(*refs, n_in_slabs, pad_rows, emit_state, n_sub):
    row_idx = tuple(range(2 * n_in_slabs)) + (2 * n_in_slabs + 11,)
    _for_subchunks(functools.partial(_mlstm_chunk, n_in_slabs=n_in_slabs, pad_rows=pad_rows,
                                     emit_state=emit_state), refs, row_idx, n_sub)


def _mlstm_chunk(*refs, n_in_slabs, pad_rows, emit_state, is_first):
    xb_refs = refs[:n_in_slabs]
    zb_refs = refs[n_in_slabs:2 * n_in_slabs]
    k = 2 * n_in_slabs
    (cw_ref, cb_ref, wq_ref, wkt_ref, wv_ref, bg_ref, nw_ref, sk_ref,
     s0_ref, m0_ref, t0_ref, y_ref) = refs[k:k + 12]
    if emit_state:
        s_st, m_st, t_st, q_scr, kt_scr, v_scr, xc_scr = refs[k + 12:]
    else:
        q_scr, kt_scr, v_scr, xc_scr, s_st, m_st, t_st = refs[k + 12:]
    c, slab_w = xb_refs[0].shape
    n_heads, dh = s_st.shape[0], s_st.shape[1]
    n_blk = wq_ref.shape[0]
    y_off = y_ref.shape[1] - n_heads * dh
    scale = dh ** -0.5

    def slab_cols(slab_refs, lo, hi):
        parts = []
        while lo < hi:
            end = min(hi, (lo // slab_w + 1) * slab_w)
            parts.append(slab_refs[lo // slab_w][:, lo % slab_w:lo % slab_w + (end - lo)])
            lo = end
        return parts[0] if len(parts) == 1 else jnp.concatenate(parts, axis=1)

    @pl.when(is_first)
    def _():
        s_st[...] = s0_ref[...]
        m_st[...] = m0_ref[...]
        t_st[...] = t0_ref[...]

    row_c = lax.broadcasted_iota(jnp.int32, (c, 1), 0)
    lane = lax.broadcasted_iota(jnp.int32, (1, LANES), 1)
    y_ref[:, :y_off] = jnp.zeros((c, y_off), y_ref.dtype)

    gates = jnp.zeros((c, 2 * LANES), F32) + bg_ref[...]
    for j in range(n_blk):
        sl = slice(j * QKV_TILE, (j + 1) * QKV_TILE)
        xf = slab_cols(xb_refs, j * QKV_TILE, (j + 1) * QKV_TILE)
        if pad_rows:
            xf = jnp.where(row_c >= pad_rows, xf, 0.0)
        xc = _causal_conv_silu(xf, t_st[:, sl], cw_ref[:, sl], cb_ref[:, sl])
        t_st[:, sl] = xf[c - SUBLANES:]
        xcb = xc.astype(BF16)
        qg = jnp.dot(xcb, wq_ref[j], preferred_element_type=F32)
        vg = jnp.dot(xf.astype(BF16), wv_ref[j], preferred_element_type=F32)
        q_scr[:, sl] = qg[:, :QKV_TILE] * scale
        kt_scr[sl, :] = _dot_nt(wkt_ref[j], xcb)
        v_scr[:, sl] = vg[:, :QKV_TILE].astype(BF16)
        xc_scr[:, sl] = xc
        gates = gates + qg[:, QKV_TILE:] + vg[:, QKV_TILE:]

    ig = gates[:, :LANES]
    lf = _log_sigmoid(gates[:, LANES:])
    if pad_rows:
        ig = jnp.where(row_c >= pad_rows, ig, -jnp.inf)
        lf = jnp.where(row_c >= pad_rows, lf, 0.0)
    causal = _causal_masks(c)
    bcum = _dot_f32(causal.astype(F32), lf)
    a = ig - bcum
    m_old = m_st[...]
    big_m = jnp.maximum(_cummax_rows(a), m_old)
    w_prev = jnp.exp(m_old - big_m)
    e_inv = jnp.exp(-(bcum + big_m))
    m_last = big_m[c - 1:c]
    decay = jnp.exp(m_old - m_last)
    e_col = jnp.exp(a - m_last)
    m_st[...] = bcum[c - 1:c] + m_last
    a2_t = (a * LOG2E).T
    m2 = big_m * LOG2E
    e_t = e_col.T

    for h in range(n_heads):
        hs = slice(h * dh, (h + 1) * dh)
        wt = jnp.exp2(jnp.where(causal, a2_t[h:h + 1, :] - m2[:, h:h + 1], -jnp.inf))
        kt = kt_scr[hs, :]
        qh = q_scr[:, hs]
        s = _dot(qh, kt) * wt
        lhs = jnp.concatenate([s.astype(BF16), (qh * w_prev[:, h:h + 1]).astype(BF16)], axis=1)
        ones_h = jnp.broadcast_to(jnp.where(lane == h, 1.0, 0.0).astype(BF16), (c, LANES))
        vext = jnp.concatenate([v_scr[:, hs], ones_h], axis=1)
        st = s_st[h]
        nd = jnp.dot(lhs, jnp.concatenate([vext, st.astype(BF16)], axis=0),
                     preferred_element_type=F32)
        r_blk = 1.0 / jnp.maximum(jnp.abs(nd[:, dh:]), e_inv)
        hh = nd[:, :dh] * r_blk[:, h:h + 1]
        wkt = (kt * e_t[h:h + 1, :]).astype(BF16)
        s_st[h] = decay[:, h:h + 1] * st + jnp.dot(wkt, vext, preferred_element_type=F32)
        mu = jnp.mean(hh, axis=1, keepdims=True)
        hc = hh - mu
        var = jnp.mean(hc * hc, axis=1, keepdims=True)
        hn = hc * lax.rsqrt(var + HEAD_NORM_EPS) * nw_ref[:, hs]
        out = (hn + sk_ref[:, hs] * xc_scr[:, hs]) * _silu_of_half(slab_cols(zb_refs, h * dh, (h + 1) * dh))
        y_ref[:, y_off + h * dh:y_off + (h + 1) * dh] = out.astype(y_ref.dtype)


def _state_specs(shapes, emit_state):
    st_in = [pl.BlockSpec((None,) + s, lambda b, i, nd=len(s): (0,) * (nd + 1),
                          pipeline_mode=pl.Buffered(1)) for s in shapes]
    if emit_state:
        st_out = [pl.BlockSpec((None,) + s, lambda b, i, nd=len(s): (b,) + (0,) * nd) for s in shapes]
        return st_in, st_out, []
    return st_in, [], [pltpu.VMEM(s, F32) for s in shapes]


def _mlstm_call(p, col_x, col_z, slab_w, out_width, wts, state0, bsz, n_chunks, pad_rows, emit_state):
    c = CHUNK
    width = wts["cw"].shape[1]
    nh = MLSTM_HEADS
    dh = width // nh
    n_slabs = width // slab_w
    n_sub, steps, row = _step_layout(n_chunks)
    slab_spec = lambda col: pl.BlockSpec((c * n_sub, slab_w), lambda b, i: (row(b, i), col))
    shapes = [(nh, dh, dh + LANES), (1, LANES), (SUBLANES, width)]
    st_in, st_out, st_scr = _state_specs(shapes, emit_state)
    names = ["cw", "cb", "wq", "wkt", "wv", "bg", "nw", "sk"]
    work = [pltpu.VMEM((c, width), F32), pltpu.VMEM((width, c), F32),
            pltpu.VMEM((c, width), BF16), pltpu.VMEM((c, width), F32)]
    return pl.pallas_call(
        functools.partial(_mlstm_kernel, n_in_slabs=n_slabs, pad_rows=pad_rows, emit_state=emit_state,
                          n_sub=n_sub),
        grid=(bsz, steps),
        in_specs=[slab_spec(col_x + s) for s in range(n_slabs)]
        + [slab_spec(col_z + s) for s in range(n_slabs)]
        + [_const_spec(wts[n].shape) for n in names] + st_in,
        out_specs=[pl.BlockSpec((c * n_sub, out_width), lambda b, i: (row(b, i), 0))] + st_out,
        out_shape=[jax.ShapeDtypeStruct((bsz * n_chunks * c, out_width), BF16)]
        + ([jax.ShapeDtypeStruct((bsz,) + s, F32) for s in shapes] if emit_state else []),
        scratch_shapes=work + st_scr,
        compiler_params=_params(2),
        name="mlstm_mixer",
    )(*([p] * (2 * n_slabs)), *[wts[n] for n in names], *state0)


def _block_diag_tiles(w, tile):
    nb, b, _ = w.shape
    rows = w.reshape(nb * b // tile, tile, b)
    col = jnp.arange(tile)
    spread = (col[None, :] % b == jnp.arange(b)[:, None]).astype(w.dtype)
    same_block = col[:, None] // b == col[None, :] // b
    return jnp.where(same_block, jnp.einsum("jrb,bc->jrc", rows, spread), 0.0)


def _mlstm_weights(conv_w, conv_b, wq, wk, wv, w_gate, b_gate, norm_w, skip):
    width = conv_w.shape[1]
    nh = MLSTM_HEADS
    n_blk = width // QKV_TILE

    def gate_cols(w_hw, w_g):
        g = jnp.einsum("nio,nog->nig", w_hw, w_g.reshape(-1, QKV_BLOCK, 2 * nh)).reshape(width, 2 * nh)
        fill = jnp.zeros((width, LANES - nh), F32)
        out = jnp.concatenate([g[:, :nh], fill, g[:, nh:], fill], axis=1)
        return out.reshape(n_blk, QKV_TILE, 2 * LANES)

    g_qk = gate_cols(wq, w_gate[:width]) + gate_cols(wk, w_gate[width:2 * width])
    g_v = gate_cols(wv, w_gate[2 * width:])
    wq_t = jnp.concatenate([_block_diag_tiles(wq, QKV_TILE), g_qk], axis=2).astype(BF16)
    wv_t = jnp.concatenate([_block_diag_tiles(wv, QKV_TILE), g_v], axis=2).astype(BF16)
    wk_t = _block_diag_tiles(wk.transpose(0, 2, 1), QKV_TILE).astype(BF16)
    bg = jnp.zeros((1, 2 * LANES), F32)
    bg = bg.at[0, :nh].set(b_gate[:nh]).at[0, LANES:LANES + nh].set(b_gate[nh:])
    return dict(cw=0.5 * conv_w, cb=0.5 * conv_b.reshape(1, width), wq=wq_t, wkt=wk_t, wv=wv_t, bg=bg,
                nw=norm_w.reshape(1, width), sk=skip.reshape(1, width))


def _ssd_kernel(*refs, pad_rows, emit_state, n_sub):
    _for_subchunks(functools.partial(_ssd_chunk, pad_rows=pad_rows, emit_state=emit_state), refs,
                   (0, 1, 2, 3, 4, 19), n_sub)


def _ssd_chunk(*refs, pad_rows, emit_state, is_first):
    (z_ref, xs_ref, bm_ref, cm_ref, dt_ref, cwx_ref, cbx_ref, cwb_ref, cbb_ref, cwc_ref, cbc_ref,
     dtb_ref, a_ref, d_ref, gn_ref, s0_ref, tx0_ref, tb0_ref, tc0_ref, y_ref) = refs[:20]
    if emit_state:
        s_st, tx_st, tb_st, tc_st, xs_scr, bm_scr, cm_scr, y_scr = refs[20:]
    else:
        xs_scr, bm_scr, cm_scr, y_scr, s_st, tx_st, tb_st, tc_st = refs[20:]
    c, inner = xs_ref.shape
    n_pairs = s_st.shape[0]
    pairs_per_group = n_pairs // SSD_GROUPS
    gw = inner // SSD_GROUPS

    @pl.when(is_first)
    def _():
        s_st[...] = s0_ref[...]
        tx_st[...] = tx0_ref[...]
        tb_st[...] = tb0_ref[...]
        tc_st[...] = tc0_ref[...]

    row_c = lax.broadcasted_iota(jnp.int32, (c, 1), 0)

    def conv_into(dst_scr, src_ref, tail_st, w_ref, b_ref):
        for j in range(src_ref.shape[1] // CONV_BLOCK):
            sl = slice(j * CONV_BLOCK, (j + 1) * CONV_BLOCK)
            x = src_ref[:, sl]
            if pad_rows:
                x = jnp.where(row_c >= pad_rows, x, 0.0)
            dst_scr[:, sl] = _causal_conv_silu(x, tail_st[:, sl], w_ref[:, sl], b_ref[:, sl])
            tail_st[:, sl] = x[c - SUBLANES:]

    conv_into(xs_scr, xs_ref, tx_st, cwx_ref, cbx_ref)
    conv_into(bm_scr, bm_ref, tb_st, cwb_ref, cbb_ref)
    conv_into(cm_scr, cm_ref, tc_st, cwc_ref, cbc_ref)

    dt = jax.nn.softplus(dt_ref[...] + dtb_ref[...])
    if pad_rows:
        dt = jnp.where(row_c >= pad_rows, dt, 0.0)
    causal = _causal_masks(c)
    cum = _dot_f32(causal.astype(F32), dt * a_ref[...])
    cum_t = cum.T
    dt_t = dt.T
    dec_t = jnp.exp(cum_t[:, c - 1:c] - cum_t) * dt_t
    src2_t = (cum_t - jnp.log(dt_t)) * LOG2E
    cum2 = cum * LOG2E
    e_last = jnp.exp(cum[c - 1:c, :])
    lo = lax.broadcasted_iota(jnp.int32, (1, LANES), 1) < SSD_HEAD_DIM

    for g in range(SSD_GROUPS):
        gs = slice(g * SSD_STATE, (g + 1) * SSD_STATE)
        cg = cm_scr[:, gs]
        bg_t = bm_scr[:, gs].T
        cb = _dot(cg, bg_t)
        for pr in range(pairs_per_group):
            pidx = g * pairs_per_group + pr
            ps = slice(pidx * LANES, (pidx + 1) * LANES)
            xp = xs_scr[:, ps]
            xpb = xp.astype(BF16)
            st = s_st[pidx]
            rhs = jnp.concatenate([xpb, st.astype(BF16)], axis=0)
            ys, ups = [], []
            for h in (2 * pidx, 2 * pidx + 1):
                bc = jnp.broadcast_to(cum2[:, h:h + 1], (c, c))
                seg = jnp.exp2(jnp.where(causal, bc - src2_t[h:h + 1, :], -jnp.inf))
                lhs = jnp.concatenate([(cb * seg).astype(BF16), (cg * jnp.exp2(bc)).astype(BF16)],
                                      axis=1)
                ys.append(jnp.dot(lhs, rhs, preferred_element_type=F32))
                ups.append(jnp.dot((bg_t * dec_t[h:h + 1, :]).astype(BF16), xpb,
                                   preferred_element_type=F32))
            h0 = 2 * pidx
            y_scr[:, ps] = jnp.where(lo, ys[0], ys[1]) + d_ref[:, ps] * xp
            el = jnp.where(lo, e_last[:, h0:h0 + 1], e_last[:, h0 + 1:h0 + 2])
            s_st[pidx] = el * st + jnp.where(lo, ups[0], ups[1])

    for g in range(SSD_GROUPS):
        gs = slice(g * gw, (g + 1) * gw)
        yg = y_scr[:, gs] * _silu_of_half(z_ref[:, gs])
        yg = yg * lax.rsqrt(jnp.mean(yg * yg, axis=1, keepdims=True) + NORM_EPS)
        y_ref[:, gs] = (yg * gn_ref[:, gs]).astype(y_ref.dtype)


def _ssd_call(p, pdt, wts, state0, bsz, n_chunks, pad_rows, emit_state):
    c = CHUNK
    inner = wts["d"].shape[1]
    gn = wts["cwb"].shape[1]
    n_pairs = inner // LANES
    n_sub, steps, row = _step_layout(n_chunks)
    cs = c * n_sub
    shapes = [(n_pairs, SSD_STATE, LANES), (SUBLANES, inner), (SUBLANES, gn), (SUBLANES, gn)]
    st_in, st_out, st_scr = _state_specs(shapes, emit_state)
    names = ["cwx", "cbx", "cwb", "cbb", "cwc", "cbc", "dtb", "a", "d", "gn"]
    bcol = 2 * inner // gn
    work = [pltpu.VMEM((c, inner), F32), pltpu.VMEM((c, gn), F32), pltpu.VMEM((c, gn), F32),
            pltpu.VMEM((c, inner), F32)]
    return pl.pallas_call(
        functools.partial(_ssd_kernel, pad_rows=pad_rows, emit_state=emit_state, n_sub=n_sub),
        grid=(bsz, steps),
        in_specs=[pl.BlockSpec((cs, inner), lambda b, i: (row(b, i), 0)),
                  pl.BlockSpec((cs, inner), lambda b, i: (row(b, i), 1)),
                  pl.BlockSpec((cs, gn), lambda b, i: (row(b, i), bcol)),
                  pl.BlockSpec((cs, gn), lambda b, i: (row(b, i), bcol + 1)),
                  pl.BlockSpec((cs, LANES), lambda b, i: (row(b, i), 0))]
        + [_const_spec(wts[n].shape) for n in names] + st_in,
        out_specs=[pl.BlockSpec((cs, inner), lambda b, i: (row(b, i), 0))] + st_out,
        out_shape=[jax.ShapeDtypeStruct((bsz * n_chunks * c, inner), BF16)]
        + ([jax.ShapeDtypeStruct((bsz,) + s, F32) for s in shapes] if emit_state else []),
        scratch_shapes=work + st_scr,
        compiler_params=_params(2),
        name="ssd_mixer",
    )(p, p, p, p, pdt, *[wts[n] for n in names], *state0)


def _ssd_weights(conv_w, conv_b, dt_bias, a_log, d, gnorm):
    n_heads = dt_bias.shape[0]
    inner = gnorm.shape[0]
    gn = (conv_w.shape[1] - inner) // 2
    pad = lambda v: jnp.zeros((1, LANES), F32).at[0, :n_heads].set(v)
    cb = 0.5 * conv_b.reshape(1, -1)
    conv_w = 0.5 * conv_w
    return dict(cwx=conv_w[:, :inner], cbx=cb[:, :inner],
                cwb=conv_w[:, inner:inner + gn], cbb=cb[:, inner:inner + gn],
                cwc=conv_w[:, inner + gn:], cbc=cb[:, inner + gn:],
                dtb=pad(dt_bias), a=pad(-jnp.exp(a_log)),
                d=jnp.repeat(d, inner // n_heads).reshape(1, inner), gn=gnorm.reshape(1, inner))


def kernel(x, meta_tokens, ab_norm, ab_w_in, s5_lambda_re, s5_lambda_im, s5_log_dt, s5_b_re, s5_b_im, s5_c_re, s5_c_im, s5_d, s5_glu_w, s5_glu_b, ml_conv_w, ml_conv_b, ml_wq, ml_wk, ml_wv, ml_w_gate, ml_b_gate, ml_norm, ml_skip, ab_w_out, ssd_norm, ssd_w_in, ssd_conv_w, ssd_conv_b, ssd_dt_bias, ssd_a_log, ssd_d, ssd_gnorm, ssd_w_out, final_norm):
    bsz, seq, d_model = x.shape
    assert seq % CHUNK == 0 and N_META <= CHUNK
    n_chunks = seq // CHUNK
    s5_w = s5_d.shape[1]
    ml_w = ml_conv_w.shape[2]
    ssd_inner = ssd_gnorm.shape[1]
    ssd_heads = ssd_dt_bias.shape[1]
    ssd_main = ssd_w_in.shape[2] - ssd_heads
    pad_rows = CHUNK - N_META

    half = lambda n: jnp.full((n,), 0.5, F32)
    one = lambda n: jnp.ones((n,), F32)
    w_in0 = (ab_w_in[0] * jnp.concatenate([one(s5_w), half(s5_w), one(ml_w), half(ml_w)])).astype(BF16)
    slab_w = s5_w
    w_out0 = ab_w_out[0].astype(BF16)
    s5_wts = _s5_weights(s5_lambda_re[0], s5_lambda_im[0], s5_log_dt[0], s5_b_re[0], s5_b_im[0],
                         s5_c_re[0], s5_c_im[0], s5_d[0], s5_glu_w[0], s5_glu_b[0])
    ml_wts = _mlstm_weights(ml_conv_w[0], ml_conv_b[0], ml_wq[0], ml_wk[0], ml_wv[0],
                            ml_w_gate[0], ml_b_gate[0], ml_norm[0], ml_skip[0])
    w_in1_t = jnp.swapaxes(ssd_w_in[0], 0, 1)
    w_in1_main = (w_in1_t[:ssd_main]
                  * jnp.concatenate([half(ssd_inner), one(ssd_main - ssd_inner)])[:, None]).astype(BF16)
    w_in1_dt = jnp.zeros((LANES, d_model), F32).at[:ssd_heads].set(w_in1_t[ssd_main:]).astype(BF16)
    w_out1 = ssd_w_out[0].astype(BF16)
    ssd_wts = _ssd_weights(ssd_conv_w[0], ssd_conv_b[0], ssd_dt_bias[0], ssd_a_log[0], ssd_d[0],
                           ssd_gnorm[0])

    n_tiles = s5_wts["apr"].shape[0]
    dh = ml_w // MLSTM_HEADS
    zeros = lambda *s: jnp.zeros(s, F32)
    s5_zero = (zeros(1, n_tiles, 1, LANES), zeros(1, n_tiles, 1, LANES))
    ml_zero = (zeros(1, MLSTM_HEADS, dh, dh + LANES), zeros(1, 1, LANES), zeros(1, SUBLANES, ml_w))
    gn = (ssd_main - 2 * ssd_inner) // 2
    ssd_zero = (zeros(1, ssd_inner // LANES, SSD_STATE, LANES), zeros(1, SUBLANES, ssd_inner),
                zeros(1, SUBLANES, gn), zeros(1, SUBLANES, gn))

    def layer0(h, b, nc, pad, s5_state, ml_state, emit_state):
        p = _norm_proj(h, ab_norm[0], w_in0, name="ab_in_proj")[0]
        y, *ml_out = _mlstm_call(p, 2 * s5_w // slab_w, (2 * s5_w + ml_w) // slab_w, slab_w,
                                 s5_w + ml_w, ml_wts, ml_state, b, nc, pad, emit_state)
        y, sr, si = _s5_call(p, 0, 1, y, 0, s5_wts, s5_state, b, nc, pad)
        h = _res_proj(y, w_out0, h, name="ab_out_proj")
        return h, (sr, si), tuple(ml_out)

    def layer1_mixer(h, b, nc, pad, ssd_state, emit_state):
        p, pdt = _norm_proj(h, ssd_norm[0], w_in1_main, w_in1_dt, w_transposed=True,
                            name="ssd_in_proj")
        y, *st = _ssd_call(p, pdt, ssd_wts, ssd_state, b, nc, pad, emit_state)
        return y, tuple(st)

    hm = jnp.concatenate([zeros(pad_rows, d_model), meta_tokens.astype(F32)], axis=0)
    hm, s5_state, ml_state = layer0(hm, 1, 1, pad_rows, s5_zero, ml_zero, True)
    _, ssd_state = layer1_mixer(hm, 1, 1, pad_rows, ssd_zero, True)

    h = x.reshape(bsz * seq, d_model)
    h, _, _ = layer0(h, bsz, n_chunks, 0, s5_state, ml_state, False)
    y, _ = layer1_mixer(h, bsz, n_chunks, 0, ssd_state, False)
    out = _res_proj_norm(y, w_out1, h, final_norm, name="ssd_out_proj_norm")
    return out.reshape(bsz, seq, d_model)
```

```python
import functools

import jax
import jax.numpy as jnp
from jax import lax
from jax.experimental import pallas as pl
from jax.experimental.pallas import tpu as pltpu

F32 = jnp.float32
BF16 = jnp.bfloat16

N_META = 16
CHUNK = 128
NORM_EPS = 1e-6
HEAD_NORM_EPS = 1e-5
LANES = 128
SUBLANES = 8
VMEM_LIMIT = 56 * 1024 * 1024

S5_GROUP_SIZE = 16
S5_STATE = 64
S5_OCT = 8
MLSTM_HEADS = 8
QKV_BLOCK = 4
QKV_TILE = 256
SSD_HEAD_DIM = 64
SSD_STATE = 128
SSD_GROUPS = 8
CONV_BLOCK = 512
LOG2E = 1.4426950408889634
NORM_PROJ_TN = 1024
SUBCHUNKS = 2


def _params(n_grid):
    return pltpu.CompilerParams(dimension_semantics=("arbitrary",) * n_grid,
                                vmem_limit_bytes=VMEM_LIMIT)


def _const_spec(shape):
    nd = len(shape)
    return pl.BlockSpec(shape, lambda *_: (0,) * nd, pipeline_mode=pl.Buffered(1))


def _dot(a, b):
    return jnp.dot(a.astype(BF16), b.astype(BF16), preferred_element_type=F32)


def _dot_nt(a, b):
    return lax.dot_general(a.astype(BF16), b.astype(BF16), (((1,), (1,)), ((), ())),
                           preferred_element_type=F32)


def _dot_f32(a, b):
    return jnp.dot(a, b, preferred_element_type=F32, precision=lax.Precision.HIGHEST)


def _sigmoid(x):
    return 0.5 * (1.0 + jnp.tanh(0.5 * x))


def _silu_of_half(h):
    return h + h * jnp.tanh(h)


def _shift_rows(x, d, prev8):
    rolled = pltpu.roll(x, d, axis=0)
    row8 = lax.broadcasted_iota(jnp.int32, (SUBLANES, x.shape[1]), 0)
    head = jnp.where(row8 < d, pltpu.roll(prev8, d, axis=0), rolled[:SUBLANES])
    if x.shape[0] == SUBLANES:
        return head
    return jnp.concatenate([head, rolled[SUBLANES:]], axis=0)


def _causal_conv_silu(x, tail8, w_half, b_half):
    assert w_half.shape[0] == 4
    w0, w1, w2, w3 = (w_half[i:i + 1, :] for i in range(4))
    x2 = _shift_rows(x, 2, tail8)
    u = x * w2 + x2 * w0
    u_tail = tail8 * w2 + pltpu.roll(tail8, 2, axis=0) * w0
    h = x * w3 + x2 * w1 + b_half + _shift_rows(u, 1, u_tail)
    return h + h * jnp.tanh(h)


def _causal_masks(c):
    r_i = lax.broadcasted_iota(jnp.int32, (c, c), 0)
    c_i = lax.broadcasted_iota(jnp.int32, (c, c), 1)
    return r_i >= c_i


def _for_subchunks(chunk_fn, refs, row_idx, n_sub):
    first_step = pl.program_id(1) == 0
    if n_sub == 1:
        chunk_fn(*refs, is_first=first_step)
        return

    def step(s, carry):
        rows = pl.ds(pl.multiple_of(s * CHUNK, CHUNK), CHUNK)
        sub = [r.at[rows, :] if k in row_idx else r for k, r in enumerate(refs)]
        chunk_fn(*sub, is_first=jnp.logical_and(first_step, s == 0))
        return carry

    lax.fori_loop(0, n_sub, step, 0)


def _step_layout(n_chunks):
    n_sub = SUBCHUNKS if n_chunks % SUBCHUNKS == 0 else 1
    steps = n_chunks // n_sub
    return n_sub, steps, lambda b, i: b * steps + i


def _norm_proj_kernel(a_ref, g_ref, w_ref, *rest, has_extra, w_transposed):
    if has_extra:
        wx_ref, o_ref, ox_ref, a_scr = rest
    else:
        o_ref, a_scr = rest
    tm = a_ref.shape[0]
    slab = min(tm, CHUNK)

    def mm(lhs, rhs):
        if w_transposed:
            return lax.dot_general(lhs, rhs, (((1,), (1,)), ((), ())), preferred_element_type=F32)
        return jnp.dot(lhs, rhs, preferred_element_type=F32)

    @pl.when(pl.program_id(1) == 0)
    def _():
        def norm_slab(i, carry):
            rows = pl.ds(pl.multiple_of(i * slab, slab), slab)
            x = a_ref[rows, :]
            y = x * lax.rsqrt(jnp.mean(x * x, axis=-1, keepdims=True) + NORM_EPS)
            a_scr[rows, :] = (y * g_ref[...]).astype(BF16)
            return carry

        lax.fori_loop(0, tm // slab, norm_slab, 0)
        if has_extra:
            ox_ref[...] = mm(a_scr[...], wx_ref[...])

    o_ref[...] = mm(a_scr[...], w_ref[...])


def _norm_proj(a, g, w, w_extra=None, w_transposed=False, name="norm_proj"):
    m, k = a.shape
    n = w.shape[0] if w_transposed else w.shape[1]
    tm = min(m, 1024)
    tn = NORM_PROJ_TN
    w_spec = (pl.BlockSpec((tn, k), lambda i, j: (j, 0)) if w_transposed
              else pl.BlockSpec((k, tn), lambda i, j: (0, j)))
    in_specs = [pl.BlockSpec((tm, k), lambda i, j: (i, 0)), _const_spec((1, k)), w_spec]
    out_specs = [pl.BlockSpec((tm, tn), lambda i, j: (i, j))]
    out_shape = [jax.ShapeDtypeStruct((m, n), F32)]
    args = [a, g.reshape(1, k), w]
    if w_extra is not None:
        nx = w_extra.shape[0] if w_transposed else w_extra.shape[1]
        in_specs.append(_const_spec(w_extra.shape))
        out_specs.append(pl.BlockSpec((tm, nx), lambda i, j: (i, 0)))
        out_shape.append(jax.ShapeDtypeStruct((m, nx), F32))
        args.append(w_extra)
    return pl.pallas_call(
        functools.partial(_norm_proj_kernel, has_extra=w_extra is not None,
                          w_transposed=w_transposed),
        grid=(m // tm, n // tn),
        in_specs=in_specs, out_specs=out_specs, out_shape=out_shape,
        scratch_shapes=[pltpu.VMEM((tm, k), BF16)],
        compiler_params=_params(2),
        name=name,
    )(*args)


def _res_proj_kernel(a_ref, w_ref, r_ref, o_ref):
    o_ref[...] = jnp.dot(a_ref[...], w_ref[...], preferred_element_type=F32) + r_ref[...]


def _res_proj(a, w, res, name="res_proj"):
    m, k = a.shape
    n = w.shape[1]
    tm = min(m, 1024)
    tn = 1024
    return pl.pallas_call(
        _res_proj_kernel,
        grid=(m // tm, n // tn),
        in_specs=[pl.BlockSpec((tm, k), lambda i, j: (i, 0)),
                  pl.BlockSpec((k, tn), lambda i, j: (0, j)),
                  pl.BlockSpec((tm, tn), lambda i, j: (i, j))],
        out_specs=pl.BlockSpec((tm, tn), lambda i, j: (i, j)),
        out_shape=jax.ShapeDtypeStruct((m, n), F32),
        compiler_params=_params(2),
        name=name,
    )(a, w, res)


def _res_proj_norm_kernel(a_ref, w_ref, r_ref, g_ref, o_ref):
    tm = a_ref.shape[0]
    slab = min(tm, CHUNK)
    for s in range(tm // slab):
        rows = slice(s * slab, (s + 1) * slab)
        x = jnp.dot(a_ref[rows, :], w_ref[...], preferred_element_type=F32) + r_ref[rows, :]
        y = x * lax.rsqrt(jnp.mean(x * x, axis=-1, keepdims=True) + NORM_EPS)
        o_ref[rows, :] = y * g_ref[...]


def _res_proj_norm(a, w, res, g, name="res_proj_norm"):
    m, k = a.shape
    n = w.shape[1]
    tm = min(m, 512)
    return pl.pallas_call(
        _res_proj_norm_kernel,
        grid=(m // tm,),
        in_specs=[pl.BlockSpec((tm, k), lambda i: (i, 0)), _const_spec((k, n)),
                  pl.BlockSpec((tm, n), lambda i: (i, 0)), _const_spec((1, n))],
        out_specs=pl.BlockSpec((tm, n), lambda i: (i, 0)),
        out_shape=jax.ShapeDtypeStruct((m, n), F32),
        compiler_params=_params(1),
        name=name,
    )(a, w, res, g.reshape(1, n))


def _s5_kernel(*refs, pad_rows, n_sub):
    _for_subchunks(functools.partial(_s5_chunk, pad_rows=pad_rows), refs, (0, 1, 12), n_sub)


def _s5_chunk(u_ref, z_ref, wb_ref, wc_ref, apr_ref, api_ref, d_ref, gw_ref, gb_ref,
              sr0_ref, si0_ref, yin_ref, y_ref, sr_ref, si_ref, out_scr, *, pad_rows, is_first):
    del yin_ref
    c = u_ref.shape[0]
    seg = c // SUBLANES
    n_tiles = apr_ref.shape[0]
    tiles_per_oct = n_tiles // S5_OCT
    half = tiles_per_oct * LANES

    @pl.when(is_first)
    def _():
        sr_ref[...] = sr0_ref[...]
        si_ref[...] = si0_ref[...]

    n_slabs = out_scr.shape[0]

    def load_permuted(ref):
        for k in range(n_slabs):
            out_scr[k] = ref[:, k * LANES:(k + 1) * LANES]
        return jnp.concatenate(
            [jnp.concatenate([out_scr[k, pl.ds(i, SUBLANES, stride=seg), :] for i in range(seg)], axis=0)
             for k in range(n_slabs)], axis=1)

    u = load_permuted(u_ref)
    if pad_rows:
        row = lax.broadcasted_iota(jnp.int32, (c, 1), 0)
        t_of_row = (row & (SUBLANES - 1)) * seg + (row >> 3)
        u = jnp.where(t_of_row >= pad_rows, u, 0.0)

    zeros8 = jnp.zeros((SUBLANES, LANES), F32)

    def cmul_add(xr, xi, pr, pi, yr, yi):
        return xr + pr * yr - pi * yi, xi + pr * yi + pi * yr

    def scan_tile(j, bur, bui):
        blk = lambda i: slice(i * SUBLANES, (i + 1) * SUBLANES)
        a1r, a1i = apr_ref[j, 0:1, :], api_ref[j, 0:1, :]
        br, bi = bur[blk(0)], bui[blk(0)]
        loc = [(br, bi)]
        for i in range(1, seg):
            br, bi = cmul_add(bur[blk(i)], bui[blk(i)], a1r, a1i, br, bi)
            loc.append((br, bi))
        fr, fi = br, bi
        for d in (1, 2, 4):
            p = seg * d - 1
            fr, fi = cmul_add(fr, fi, apr_ref[j, p:p + 1, :], api_ref[j, p:p + 1, :],
                              _shift_rows(fr, d, zeros8), _shift_rows(fi, d, zeros8))
        cr, ci = sr_ref[j], si_ref[j]
        fr, fi = cmul_add(fr, fi, apr_ref[j, pl.ds(seg - 1, SUBLANES, stride=seg), :],
                          api_ref[j, pl.ds(seg - 1, SUBLANES, stride=seg), :], cr, ci)
        gr = _shift_rows(fr, 1, jnp.broadcast_to(cr, (SUBLANES, LANES)))
        gi = _shift_rows(fi, 1, jnp.broadcast_to(ci, (SUBLANES, LANES)))
        xs = [cmul_add(loc[i][0], loc[i][1], apr_ref[j, i:i + 1, :], api_ref[j, i:i + 1, :], gr, gi)
              for i in range(seg)]
        sr_ref[j] = fr[SUBLANES - 1:SUBLANES]
        si_ref[j] = fi[SUBLANES - 1:SUBLANES]
        return (jnp.concatenate([x[0] for x in xs], axis=0),
                jnp.concatenate([x[1] for x in xs], axis=0))

    ys = []
    for k in range(S5_OCT):
        bu = _dot(u[:, k * LANES:(k + 1) * LANES], wb_ref[k])
        xrs, xis = [], []
        for q in range(tiles_per_oct):
            xr, xi = scan_tile(k * tiles_per_oct + q, bu[:, q * LANES:(q + 1) * LANES],
                               bu[:, half + q * LANES:half + (q + 1) * LANES])
            xrs.append(xr)
            xis.append(xi)
        ys.append(_dot(jnp.concatenate(xrs + xis, axis=1), wc_ref[k]))
    y = jnp.concatenate(ys, axis=1) + d_ref[...] * u
    g = jax.nn.gelu(y)
    gate = _sigmoid(_dot(g, gw_ref[...]) + gb_ref[...])
    res = g * gate
    for k in range(n_slabs):
        cols = slice(k * LANES, (k + 1) * LANES)
        for i in range(seg):
            out_scr[k, pl.ds(i, SUBLANES, stride=seg), :] = res[i * SUBLANES:(i + 1) * SUBLANES, cols]
        y_ref[:, cols] = (out_scr[k] * _silu_of_half(z_ref[:, cols])).astype(y_ref.dtype)


def _s5_call(p, col_u, col_z, y_buf, col_y, wts, state0, bsz, n_chunks, pad_rows):
    c = CHUNK
    width = wts["d"].shape[1]
    n_tiles = wts["apr"].shape[0]
    n_sub, steps, row = _step_layout(n_chunks)
    st_spec_in = pl.BlockSpec((None, n_tiles, 1, LANES), lambda b, i: (0, 0, 0, 0),
                              pipeline_mode=pl.Buffered(1))
    st_spec_out = pl.BlockSpec((None, n_tiles, 1, LANES), lambda b, i: (b, 0, 0, 0))
    st_shape = jax.ShapeDtypeStruct((bsz, n_tiles, 1, LANES), F32)
    names = ["wb", "wc", "apr", "api", "d", "gw", "gb"]
    return pl.pallas_call(
        functools.partial(_s5_kernel, pad_rows=pad_rows, n_sub=n_sub),
        grid=(bsz, steps),
        in_specs=[pl.BlockSpec((c * n_sub, width), lambda b, i: (row(b, i), col_u)),
                  pl.BlockSpec((c * n_sub, width), lambda b, i: (row(b, i), col_z))]
        + [_const_spec(wts[n].shape) for n in names]
        + [st_spec_in, st_spec_in, pl.BlockSpec(memory_space=pl.ANY)],
        out_specs=[pl.BlockSpec((c * n_sub, width), lambda b, i: (row(b, i), col_y)),
                   st_spec_out, st_spec_out],
        out_shape=[jax.ShapeDtypeStruct(y_buf.shape, y_buf.dtype), st_shape, st_shape],
        scratch_shapes=[pltpu.VMEM((width // LANES, c, LANES), F32)],
        input_output_aliases={len(names) + 4: 0},
        compiler_params=_params(2),
        name="s5_mixer",
    )(p, p, *[wts[n] for n in names], state0[0], state0[1], y_buf)


def _s5_weights(lam_re, lam_im, log_dt, b_re, b_im, c_re, c_im, d, glu_w, glu_b):
    g_n, p_n = lam_re.shape
    dt = jnp.exp(log_dt)[:, None]
    mag = jnp.exp(lam_re * dt)
    ar, ai = mag * jnp.cos(lam_im * dt), mag * jnp.sin(lam_im * dt)
    den = lam_re * lam_re + lam_im * lam_im
    qr = ((ar - 1.0) * lam_re + ai * lam_im) / den
    qi = (ai * lam_re - (ar - 1.0) * lam_im) / den
    bbr = qr[..., None] * b_re - qi[..., None] * b_im
    bbi = qr[..., None] * b_im + qi[..., None] * b_re
    n_oct = g_n // S5_OCT
    eye = jnp.eye(S5_OCT, dtype=F32)

    def in_proj(w):
        w = w.reshape(n_oct, S5_OCT, p_n, S5_GROUP_SIZE)
        return jnp.einsum("kgph,gG->kghGp", w, eye).reshape(n_oct, S5_OCT * S5_GROUP_SIZE, S5_OCT * p_n)

    def out_proj(w):
        w = w.reshape(n_oct, S5_OCT, S5_GROUP_SIZE, p_n)
        return jnp.einsum("kghp,gG->kgpGh", w, eye).reshape(n_oct, S5_OCT * p_n, S5_OCT * S5_GROUP_SIZE)

    wb = jnp.concatenate([in_proj(bbr), in_proj(bbi)], axis=2).astype(BF16)
    wc = jnp.concatenate([out_proj(c_re), out_proj(-c_im)], axis=1).astype(BF16)
    pr, pi = ar.reshape(1, -1), ai.reshape(1, -1)
    while pr.shape[0] < CHUNK:
        lr_, li_ = pr[-1:], pi[-1:]
        pr, pi = (jnp.concatenate([pr, pr * lr_ - pi * li_], axis=0),
                  jnp.concatenate([pi, pr * li_ + pi * lr_], axis=0))
    n_tiles = g_n * p_n // LANES
    tile = lambda a: a.reshape(CHUNK, n_tiles, LANES).transpose(1, 0, 2)
    width = d.shape[0]
    return dict(wb=wb, wc=wc, apr=tile(pr), api=tile(pi), d=d.reshape(1, width),
                gw=glu_w.astype(BF16), gb=glu_b.reshape(1, width))


def _log_sigmoid(x):
    return jnp.minimum(x, 0.0) - jnp.log1p(jnp.exp(-jnp.abs(x)))


def _cummax_rows(x):
    c = x.shape[0]
    neg8 = jnp.full((SUBLANES, x.shape[1]), -jnp.inf, F32)
    d = 1
    while d < c:
        if d < SUBLANES:
            sh = _shift_rows(x, d, neg8)
        else:
            sh = jnp.concatenate([jnp.full((d, x.shape[1]), -jnp.inf, F32), x[:c - d]], axis=0)
        x = jnp.maximum(x, sh)
        d *= 2
    return x


def _mlstm_kernel(*refs, n_in_slabs, pad_rows, emit_state, n_sub):
    row_idx = tuple(range(2 * n_in_slabs)) + (2 * n_in_slabs + 11,)
    _for_subchunks(functools.partial(_mlstm_chunk, n_in_slabs=n_in_slabs, pad_rows=pad_rows,
                                     emit_state=emit_state), refs, row_idx, n_sub)


def _mlstm_chunk(*refs, n_in_slabs, pad_rows, emit_state, is_first):
    xb_refs = refs[:n_in_slabs]
    zb_refs = refs[n_in_slabs:2 * n_in_slabs]
    k = 2 * n_in_slabs
    (cw_ref, cb_ref, wq_ref, wkt_ref, wv_ref, bg_ref, nw_ref, sk_ref,
     s0_ref, m0_ref, t0_ref, y_ref) = refs[k:k + 12]
    if emit_state:
        s_st, m_st, t_st, q_scr, kt_scr, v_scr, xc_scr = refs[k + 12:]
    else:
        q_scr, kt_scr, v_scr, xc_scr, s_st, m_st, t_st = refs[k + 12:]
    c, slab_w = xb_refs[0].shape
    n_heads, dh = s_st.shape[0], s_st.shape[1]
    n_blk = wq_ref.shape[0]
    y_off = y_ref.shape[1] - n_heads * dh
    scale = dh ** -0.5

    def slab_cols(slab_refs, lo, hi):
        parts = []
        while lo < hi:
            end = min(hi, (lo // slab_w + 1) * slab_w)
            parts.append(slab_refs[lo // slab_w][:, lo % slab_w:lo % slab_w + (end - lo)])
            lo = end
        return parts[0] if len(parts) == 1 else jnp.concatenate(parts, axis=1)

    @pl.when(is_first)
    def _():
        s_st[...] = s0_ref[...]
        m_st[...] = m0_ref[...]
        t_st[...] = t0_ref[...]

    row_c = lax.broadcasted_iota(jnp.int32, (c, 1), 0)
    lane = lax.broadcasted_iota(jnp.int32, (1, LANES), 1)
    y_ref[:, :y_off] = jnp.zeros((c, y_off), y_ref.dtype)

    gates = jnp.zeros((c, 2 * LANES), F32) + bg_ref[...]
    for j in range(n_blk):
        sl = slice(j * QKV_TILE, (j + 1) * QKV_TILE)
        xf = slab_cols(xb_refs, j * QKV_TILE, (j + 1) * QKV_TILE)
        if pad_rows:
            xf = jnp.where(row_c >= pad_rows, xf, 0.0)
        xc = _causal_conv_silu(xf, t_st[:, sl], cw_ref[:, sl], cb_ref[:, sl])
        t_st[:, sl] = xf[c - SUBLANES:]
        xcb = xc.astype(BF16)
        qg = jnp.dot(xcb, wq_ref[j], preferred_element_type=F32)
        vg = jnp.dot(xf.astype(BF16), wv_ref[j], preferred_element_type=F32)
        q_scr[:, sl] = qg[:, :QKV_TILE] * scale
        kt_scr[sl, :] = _dot_nt(wkt_ref[j], xcb)
        v_scr[:, sl] = vg[:, :QKV_TILE].astype(BF16)
        xc_scr[:, sl] = xc
        gates = gates + qg[:, QKV_TILE:] + vg[:, QKV_TILE:]

    ig = gates[:, :LANES]
    lf = _log_sigmoid(gates[:, LANES:])
    if pad_rows:
        ig = jnp.where(row_c >= pad_rows, ig, -jnp.inf)
        lf = jnp.where(row_c >= pad_rows, lf, 0.0)
    causal = _causal_masks(c)
    bcum = _dot_f32(causal.astype(F32), lf)
    a = ig - bcum
    m_old = m_st[...]
    big_m = jnp.maximum(_cummax_rows(a), m_old)
    w_prev = jnp.exp(m_old - big_m)
    e_inv = jnp.exp(-(bcum + big_m))
    m_last = big_m[c - 1:c]
    decay = jnp.exp(m_old - m_last)
    e_col = jnp.exp(a - m_last)
    m_st[...] = bcum[c - 1:c] + m_last
    a2_t = (a * LOG2E).T
    m2 = big_m * LOG2E
    e_t = e_col.T

    for h in range(n_heads):
        hs = slice(h * dh, (h + 1) * dh)
        wt = jnp.exp2(jnp.where(causal, a2_t[h:h + 1, :] - m2[:, h:h + 1], -jnp.inf))
        kt = kt_scr[hs, :]
        qh = q_scr[:, hs]
        s = _dot(qh, kt) * wt
        lhs = jnp.concatenate([s.astype(BF16), (qh * w_prev[:, h:h + 1]).astype(BF16)], axis=1)
        ones_h = jnp.broadcast_to(jnp.where(lane == h, 1.0, 0.0).astype(BF16), (c, LANES))
        vext = jnp.concatenate([v_scr[:, hs], ones_h], axis=1)
        st = s_st[h]
        nd = jnp.dot(lhs, jnp.concatenate([vext, st.astype(BF16)], axis=0),
                     preferred_element_type=F32)
        r_blk = 1.0 / jnp.maximum(jnp.abs(nd[:, dh:]), e_inv)
        hh = nd[:, :dh] * r_blk[:, h:h + 1]
        wkt = (kt * e_t[h:h + 1, :]).astype(BF16)
        s_st[h] = decay[:, h:h + 1] * st + jnp.dot(wkt, vext, preferred_element_type=F32)
        mu = jnp.mean(hh, axis=1, keepdims=True)
        hc = hh - mu
        var = jnp.mean(hc * hc, axis=1, keepdims=True)
        hn = hc * lax.rsqrt(var + HEAD_NORM_EPS) * nw_ref[:, hs]
        out = (hn + sk_ref[:, hs] * xc_scr[:, hs]) * _silu_of_half(slab_cols(zb_refs, h * dh, (h + 1) * dh))
        y_ref[:, y_off + h * dh:y_off + (h + 1) * dh] = out.astype(y_ref.dtype)


def _state_specs(shapes, emit_state):
    st_in = [pl.BlockSpec((None,) + s, lambda b, i, nd=len(s): (0,) * (nd + 1),
                          pipeline_mode=pl.Buffered(1)) for s in shapes]
    if emit_state:
        st_out = [pl.BlockSpec((None,) + s, lambda b, i, nd=len(s): (b,) + (0,) * nd) for s in shapes]
        return st_in, st_out, []
    return st_in, [], [pltpu.VMEM(s, F32) for s in shapes]


def _mlstm_call(p, col_x, col_z, slab_w, out_width, wts, state0, bsz, n_chunks, pad_rows, emit_state):
    c = CHUNK
    width = wts["cw"].shape[1]
    nh = MLSTM_HEADS
    dh = width // nh
    n_slabs = width // slab_w
    n_sub, steps, row = _step_layout(n_chunks)
    slab_spec = lambda col: pl.BlockSpec((c * n_sub, slab_w), lambda b, i: (row(b, i), col))
    shapes = [(nh, dh, dh + LANES), (1, LANES), (SUBLANES, width)]
    st_in, st_out, st_scr = _state_specs(shapes, emit_state)
    names = ["cw", "cb", "wq", "wkt", "wv", "bg", "nw", "sk"]
    work = [pltpu.VMEM((c, width), F32), pltpu.VMEM((width, c), F32),
            pltpu.VMEM((c, width), BF16), pltpu.VMEM((c, width), F32)]
    return pl.pallas_call(
        functools.partial(_mlstm_kernel, n_in_slabs=n_slabs, pad_rows=pad_rows, emit_state=emit_state,
                          n_sub=n_sub),
        grid=(bsz, steps),
        in_specs=[slab_spec(col_x + s) for s in range(n_slabs)]
        + [slab_spec(col_z + s) for s in range(n_slabs)]
        + [_const_spec(wts[n].shape) for n in names] + st_in,
        out_specs=[pl.BlockSpec((c * n_sub, out_width), lambda b, i: (row(b, i), 0))] + st_out,
        out_shape=[jax.ShapeDtypeStruct((bsz * n_chunks * c, out_width), BF16)]
        + ([jax.ShapeDtypeStruct((bsz,) + s, F32) for s in shapes] if emit_state else []),
        scratch_shapes=work + st_scr,
        compiler_params=_params(2),
        name="mlstm_mixer",
    )(*([p] * (2 * n_slabs)), *[wts[n] for n in names], *state0)


def _block_diag_tiles(w, tile):
    nb, b, _ = w.shape
    rows = w.reshape(nb * b // tile, tile, b)
    col = jnp.arange(tile)
    spread = (col[None, :] % b == jnp.arange(b)[:, None]).astype(w.dtype)
    same_block = col[:, None] // b == col[None, :] // b
    return jnp.where(same_block, jnp.einsum("jrb,bc->jrc", rows, spread), 0.0)


def _mlstm_weights(conv_w, conv_b, wq, wk, wv, w_gate, b_gate, norm_w, skip):
    width = conv_w.shape[1]
    nh = MLSTM_HEADS
    n_blk = width // QKV_TILE

    def gate_cols(w_hw, w_g):
        g = jnp.einsum("nio,nog->nig", w_hw, w_g.reshape(-1, QKV_BLOCK, 2 * nh)).reshape(width, 2 * nh)
        fill = jnp.zeros((width, LANES - nh), F32)
        out = jnp.concatenate([g[:, :nh], fill, g[:, nh:], fill], axis=1)
        return out.reshape(n_blk, QKV_TILE, 2 * LANES)

    g_qk = gate_cols(wq, w_gate[:width]) + gate_cols(wk, w_gate[width:2 * width])
    g_v = gate_cols(wv, w_gate[2 * width:])
    wq_t = jnp.concatenate([_block_diag_tiles(wq, QKV_TILE), g_qk], axis=2).astype(BF16)
    wv_t = jnp.concatenate([_block_diag_tiles(wv, QKV_TILE), g_v], axis=2).astype(BF16)
    wk_t = _block_diag_tiles(wk.transpose(0, 2, 1), QKV_TILE).astype(BF16)
    bg = jnp.zeros((1, 2 * LANES), F32)
    bg = bg.at[0, :nh].set(b_gate[:nh]).at[0, LANES:LANES + nh].set(b_gate[nh:])
    return dict(cw=0.5 * conv_w, cb=0.5 * conv_b.reshape(1, width), wq=wq_t, wkt=wk_t, wv=wv_t, bg=bg,
                nw=norm_w.reshape(1, width), sk=skip.reshape(1, width))


def _ssd_kernel(*refs, pad_rows, emit_state, n_sub):
    _for_subchunks(functools.partial(_ssd_chunk, pad_rows=pad_rows, emit_state=emit_state), refs,
                   (0, 1, 2, 3, 4, 19), n_sub)


def _ssd_chunk(*refs, pad_rows, emit_state, is_first):
    (z_ref, xs_ref, bm_ref, cm_ref, dt_ref, cwx_ref, cbx_ref, cwb_ref, cbb_ref, cwc_ref, cbc_ref,
     dtb_ref, a_ref, d_ref, gn_ref, s0_ref, tx0_ref, tb0_ref, tc0_ref, y_ref) = refs[:20]
    if emit_state:
        s_st, tx_st, tb_st, tc_st, xs_scr, bm_scr, cm_scr, y_scr = refs[20:]
    else:
        xs_scr, bm_scr, cm_scr, y_scr, s_st, tx_st, tb_st, tc_st = refs[20:]
    c, inner = xs_ref.shape
    n_pairs = s_st.shape[0]
    pairs_per_group = n_pairs // SSD_GROUPS
    gw = inner // SSD_GROUPS

    @pl.when(is_first)
    def _():
        s_st[...] = s0_ref[...]
        tx_st[...] = tx0_ref[...]
        tb_st[...] = tb0_ref[...]
        tc_st[...] = tc0_ref[...]

    row_c = lax.broadcasted_iota(jnp.int32, (c, 1), 0)

    def conv_into(dst_scr, src_ref, tail_st, w_ref, b_ref):
        for j in range(src_ref.shape[1] // CONV_BLOCK):
            sl = slice(j * CONV_BLOCK, (j + 1) * CONV_BLOCK)
            x = src_ref[:, sl]
            if pad_rows:
                x = jnp.where(row_c >= pad_rows, x, 0.0)
            dst_scr[:, sl] = _causal_conv_silu(x, tail_st[:, sl], w_ref[:, sl], b_ref[:, sl])
            tail_st[:, sl] = x[c - SUBLANES:]

    conv_into(xs_scr, xs_ref, tx_st, cwx_ref, cbx_ref)
    conv_into(bm_scr, bm_ref, tb_st, cwb_ref, cbb_ref)
    conv_into(cm_scr, cm_ref, tc_st, cwc_ref, cbc_ref)

    dt = jax.nn.softplus(dt_ref[...] + dtb_ref[...])
    if pad_rows:
        dt = jnp.where(row_c >= pad_rows, dt, 0.0)
    causal = _causal_masks(c)
    cum = _dot_f32(causal.astype(F32), dt * a_ref[...])
    cum_t = cum.T
    dt_t = dt.T
    dec = jnp.exp(cum[c - 1:c, :] - cum) * dt
    src2_t = (cum_t - jnp.log(dt_t)) * LOG2E
    cum2 = cum * LOG2E
    e_last = jnp.exp(cum[c - 1:c, :])
    lo = lax.broadcasted_iota(jnp.int32, (1, LANES), 1) < SSD_HEAD_DIM

    for g in range(SSD_GROUPS):
        gs = slice(g * SSD_STATE, (g + 1) * SSD_STATE)
        cg = cm_scr[:, gs]
        bg_tb = bm_scr[:, gs].T.astype(BF16)
        cb = jnp.dot(cg.astype(BF16), bg_tb, preferred_element_type=F32)
        for pr in range(pairs_per_group):
            pidx = g * pairs_per_group + pr
            ps = slice(pidx * LANES, (pidx + 1) * LANES)
            xp = xs_scr[:, ps]
            xpb = xp.astype(BF16)
            st = s_st[pidx]
            rhs = jnp.concatenate([xpb, st.astype(BF16)], axis=0)
            ys = []
            for h in (2 * pidx, 2 * pidx + 1):
                bc = jnp.broadcast_to(cum2[:, h:h + 1], (c, c))
                seg = jnp.exp2(jnp.where(causal, bc - src2_t[h:h + 1, :], -jnp.inf))
                lhs = jnp.concatenate([(cb * seg).astype(BF16), (cg * jnp.exp2(bc)).astype(BF16)],
                                      axis=1)
                ys.append(jnp.dot(lhs, rhs, preferred_element_type=F32))
            h0 = 2 * pidx
            y_scr[:, ps] = jnp.where(lo, ys[0], ys[1]) + d_ref[:, ps] * xp
            el = jnp.where(lo, e_last[:, h0:h0 + 1], e_last[:, h0 + 1:h0 + 2])
            xdec = xp * jnp.where(lo, dec[:, h0:h0 + 1], dec[:, h0 + 1:h0 + 2])
            s_st[pidx] = el * st + jnp.dot(bg_tb, xdec.astype(BF16), preferred_element_type=F32)

    for g in range(SSD_GROUPS):
        gs = slice(g * gw, (g + 1) * gw)
        yg = y_scr[:, gs] * _silu_of_half(z_ref[:, gs])
        yg = yg * lax.rsqrt(jnp.mean(yg * yg, axis=1, keepdims=True) + NORM_EPS)
        y_ref[:, gs] = (yg * gn_ref[:, gs]).astype(y_ref.dtype)


def _ssd_call(p, pdt, wts, state0, bsz, n_chunks, pad_rows, emit_state):
    c = CHUNK
    inner = wts["d"].shape[1]
    gn = wts["cwb"].shape[1]
    n_pairs = inner // LANES
    n_sub, steps, row = _step_layout(n_chunks)
    cs = c * n_sub
    shapes = [(n_pairs, SSD_STATE, LANES), (SUBLANES, inner), (SUBLANES, gn), (SUBLANES, gn)]
    st_in, st_out, st_scr = _state_specs(shapes, emit_state)
    names = ["cwx", "cbx", "cwb", "cbb", "cwc", "cbc", "dtb", "a", "d", "gn"]
    bcol = 2 * inner // gn
    work = [pltpu.VMEM((c, inner), F32), pltpu.VMEM((c, gn), F32), pltpu.VMEM((c, gn), F32),
            pltpu.VMEM((c, inner), F32)]
    return pl.pallas_call(
        functools.partial(_ssd_kernel, pad_rows=pad_rows, emit_state=emit_state, n_sub=n_sub),
        grid=(bsz, steps),
        in_specs=[pl.BlockSpec((cs, inner), lambda b, i: (row(b, i), 0)),
                  pl.BlockSpec((cs, inner), lambda b, i: (row(b, i), 1)),
                  pl.BlockSpec((cs, gn), lambda b, i: (row(b, i), bcol)),
                  pl.BlockSpec((cs, gn), lambda b, i: (row(b, i), bcol + 1)),
                  pl.BlockSpec((cs, LANES), lambda b, i: (row(b, i), 0))]
        + [_const_spec(wts[n].shape) for n in names] + st_in,
        out_specs=[pl.BlockSpec((cs, inner), lambda b, i: (row(b, i), 0))] + st_out,
        out_shape=[jax.ShapeDtypeStruct((bsz * n_chunks * c, inner), BF16)]
        + ([jax.ShapeDtypeStruct((bsz,) + s, F32) for s in shapes] if emit_state else []),
        scratch_shapes=work + st_scr,
        compiler_params=_params(2),
        name="ssd_mixer",
    )(p, p, p, p, pdt, *[wts[n] for n in names], *state0)


def _ssd_weights(conv_w, conv_b, dt_bias, a_log, d, gnorm):
    n_heads = dt_bias.shape[0]
    inner = gnorm.shape[0]
    gn = (conv_w.shape[1] - inner) // 2
    pad = lambda v: jnp.zeros((1, LANES), F32).at[0, :n_heads].set(v)
    cb = 0.5 * conv_b.reshape(1, -1)
    conv_w = 0.5 * conv_w
    return dict(cwx=conv_w[:, :inner], cbx=cb[:, :inner],
                cwb=conv_w[:, inner:inner + gn], cbb=cb[:, inner:inner + gn],
                cwc=conv_w[:, inner + gn:], cbc=cb[:, inner + gn:],
                dtb=pad(dt_bias), a=pad(-jnp.exp(a_log)),
                d=jnp.repeat(d, inner // n_heads).reshape(1, inner), gn=gnorm.reshape(1, inner))


def kernel(x, meta_tokens, ab_norm, ab_w_in, s5_lambda_re, s5_lambda_im, s5_log_dt, s5_b_re, s5_b_im, s5_c_re, s5_c_im, s5_d, s5_glu_w, s5_glu_b, ml_conv_w, ml_conv_b, ml_wq, ml_wk, ml_wv, ml_w_gate, ml_b_gate, ml_norm, ml_skip, ab_w_out, ssd_norm, ssd_w_in, ssd_conv_w, ssd_conv_b, ssd_dt_bias, ssd_a_log, ssd_d, ssd_gnorm, ssd_w_out, final_norm):
    bsz, seq, d_model = x.shape
    assert seq % CHUNK == 0 and N_META <= CHUNK
    n_chunks = seq // CHUNK
    s5_w = s5_d.shape[1]
    ml_w = ml_conv_w.shape[2]
    ssd_inner = ssd_gnorm.shape[1]
    ssd_heads = ssd_dt_bias.shape[1]
    ssd_main = ssd_w_in.shape[2] - ssd_heads
    pad_rows = CHUNK - N_META

    half = lambda n: jnp.full((n,), 0.5, F32)
    one = lambda n: jnp.ones((n,), F32)
    w_in0 = (ab_w_in[0] * jnp.concatenate([one(s5_w), half(s5_w), one(ml_w), half(ml_w)])).astype(BF16)
    slab_w = s5_w
    w_out0 = ab_w_out[0].astype(BF16)
    s5_wts = _s5_weights(s5_lambda_re[0], s5_lambda_im[0], s5_log_dt[0], s5_b_re[0], s5_b_im[0],
                         s5_c_re[0], s5_c_im[0], s5_d[0], s5_glu_w[0], s5_glu_b[0])
    ml_wts = _mlstm_weights(ml_conv_w[0], ml_conv_b[0], ml_wq[0], ml_wk[0], ml_wv[0],
                            ml_w_gate[0], ml_b_gate[0], ml_norm[0], ml_skip[0])
    w_in1_t = jnp.swapaxes(ssd_w_in[0], 0, 1)
    w_in1_main = (w_in1_t[:ssd_main]
                  * jnp.concatenate([half(ssd_inner), one(ssd_main - ssd_inner)])[:, None]).astype(BF16)
    w_in1_dt = jnp.zeros((LANES, d_model), F32).at[:ssd_heads].set(w_in1_t[ssd_main:]).astype(BF16)
    w_out1 = ssd_w_out[0].astype(BF16)
    ssd_wts = _ssd_weights(ssd_conv_w[0], ssd_conv_b[0], ssd_dt_bias[0], ssd_a_log[0], ssd_d[0],
                           ssd_gnorm[0])

    n_tiles = s5_wts["apr"].shape[0]
    dh = ml_w // MLSTM_HEADS
    zeros = lambda *s: jnp.zeros(s, F32)
    s5_zero = (zeros(1, n_tiles, 1, LANES), zeros(1, n_tiles, 1, LANES))
    ml_zero = (zeros(1, MLSTM_HEADS, dh, dh + LANES), zeros(1, 1, LANES), zeros(1, SUBLANES, ml_w))
    gn = (ssd_main - 2 * ssd_inner) // 2
    ssd_zero = (zeros(1, ssd_inner // LANES, SSD_STATE, LANES), zeros(1, SUBLANES, ssd_inner),
                zeros(1, SUBLANES, gn), zeros(1, SUBLANES, gn))

    def layer0(h, b, nc, pad, s5_state, ml_state, emit_state):
        p = _norm_proj(h, ab_norm[0], w_in0, name="ab_in_proj")[0]
        y, *ml_out = _mlstm_call(p, 2 * s5_w // slab_w, (2 * s5_w + ml_w) // slab_w, slab_w,
                                 s5_w + ml_w, ml_wts, ml_state, b, nc, pad, emit_state)
        y, sr, si = _s5_call(p, 0, 1, y, 0, s5_wts, s5_state, b, nc, pad)
        h = _res_proj(y, w_out0, h, name="ab_out_proj")
        return h, (sr, si), tuple(ml_out)

    def layer1_mixer(h, b, nc, pad, ssd_state, emit_state):
        p, pdt = _norm_proj(h, ssd_norm[0], w_in1_main, w_in1_dt, w_transposed=True,
                            name="ssd_in_proj")
        y, *st = _ssd_call(p, pdt, ssd_wts, ssd_state, b, nc, pad, emit_state)
        return y, tuple(st)

    hm = jnp.concatenate([zeros(pad_rows, d_model), meta_tokens.astype(F32)], axis=0)
    hm, s5_state, ml_state = layer0(hm, 1, 1, pad_rows, s5_zero, ml_zero, True)
    _, ssd_state = layer1_mixer(hm, 1, 1, pad_rows, ssd_zero, True)

    h = x.reshape(bsz * seq, d_model)
    h, _, _ = layer0(h, bsz, n_chunks, 0, s5_state, ml_state, False)
    y, _ = layer1_mixer(h, bsz, n_chunks, 0, ssd_state, False)
    out = _res_proj_norm(y, w_out1, h, final_norm, name="ssd_out_proj_norm")
    return out.reshape(bsz, seq, d_model)
```

```python
import functools

import jax
import jax.numpy as jnp
from jax import lax
from jax.experimental import pallas as pl
from jax.experimental.pallas import tpu as pltpu

F32 = jnp.float32
BF16 = jnp.bfloat16

N_META = 16
CHUNK = 128
NORM_EPS = 1e-6
HEAD_NORM_EPS = 1e-5
LANES = 128
SUBLANES = 8
VMEM_LIMIT = 56 * 1024 * 1024

S5_GROUP_SIZE = 16
S5_STATE = 64
S5_OCT = 8
MLSTM_HEADS = 8
QKV_BLOCK = 4
QKV_TILE = 256
SSD_HEAD_DIM = 64
SSD_STATE = 128
SSD_GROUPS = 8
CONV_BLOCK = 512
LOG2E = 1.4426950408889634
NORM_PROJ_TN = 1024
SUBCHUNKS = 2


def _params(n_grid):
    return pltpu.CompilerParams(dimension_semantics=("arbitrary",) * n_grid,
                                vmem_limit_bytes=VMEM_LIMIT)


def _const_spec(shape):
    nd = len(shape)
    return pl.BlockSpec(shape, lambda *_: (0,) * nd, pipeline_mode=pl.Buffered(1))


def _dot(a, b):
    return jnp.dot(a.astype(BF16), b.astype(BF16), preferred_element_type=F32)


def _dot_nt(a, b):
    return lax.dot_general(a.astype(BF16), b.astype(BF16), (((1,), (1,)), ((), ())),
                           preferred_element_type=F32)


def _dot_f32(a, b):
    return jnp.dot(a, b, preferred_element_type=F32, precision=lax.Precision.HIGHEST)


def _sigmoid(x):
    return 0.5 * (1.0 + jnp.tanh(0.5 * x))


def _silu_of_half(h):
    return h + h * jnp.tanh(h)


def _shift_rows(x, d, prev8):
    rolled = pltpu.roll(x, d, axis=0)
    row8 = lax.broadcasted_iota(jnp.int32, (SUBLANES, x.shape[1]), 0)
    head = jnp.where(row8 < d, pltpu.roll(prev8, d, axis=0), rolled[:SUBLANES])
    if x.shape[0] == SUBLANES:
        return head
    return jnp.concatenate([head, rolled[SUBLANES:]], axis=0)


def _causal_conv_silu(x, tail8, w_half, b_half):
    assert w_half.shape[0] == 4
    w0, w1, w2, w3 = (w_half[i:i + 1, :] for i in range(4))
    x2 = _shift_rows(x, 2, tail8)
    u = x * w2 + x2 * w0
    u_tail = tail8 * w2 + pltpu.roll(tail8, 2, axis=0) * w0
    h = x * w3 + x2 * w1 + b_half + _shift_rows(u, 1, u_tail)
    return h + h * jnp.tanh(h)


def _causal_masks(c):
    r_i = lax.broadcasted_iota(jnp.int32, (c, c), 0)
    c_i = lax.broadcasted_iota(jnp.int32, (c, c), 1)
    return r_i >= c_i


def _for_subchunks(chunk_fn, refs, row_idx, n_sub):
    first_step = pl.program_id(1) == 0
    if n_sub == 1:
        chunk_fn(*refs, is_first=first_step)
        return

    def step(s, carry):
        rows = pl.ds(pl.multiple_of(s * CHUNK, CHUNK), CHUNK)
        sub = [r.at[rows, :] if k in row_idx else r for k, r in enumerate(refs)]
        chunk_fn(*sub, is_first=jnp.logical_and(first_step, s == 0))
        return carry

    lax.fori_loop(0, n_sub, step, 0)


def _step_layout(n_chunks):
    n_sub = SUBCHUNKS if n_chunks % SUBCHUNKS == 0 else 1
    steps = n_chunks // n_sub
    return n_sub, steps, lambda b, i: b * steps + i


def _norm_proj_kernel(a_ref, g_ref, w_ref, *rest, has_extra, w_transposed):
    if has_extra:
        wx_ref, o_ref, ox_ref, a_scr = rest
    else:
        o_ref, a_scr = rest
    tm = a_ref.shape[0]
    slab = min(tm, CHUNK)

    def mm(lhs, rhs):
        if w_transposed:
            return lax.dot_general(lhs, rhs, (((1,), (1,)), ((), ())), preferred_element_type=F32)
        return jnp.dot(lhs, rhs, preferred_element_type=F32)

    @pl.when(pl.program_id(1) == 0)
    def _():
        def norm_slab(i, carry):
            rows = pl.ds(pl.multiple_of(i * slab, slab), slab)
            x = a_ref[rows, :]
            y = x * lax.rsqrt(jnp.mean(x * x, axis=-1, keepdims=True) + NORM_EPS)
            a_scr[rows, :] = (y * g_ref[...]).astype(BF16)
            return carry

        lax.fori_loop(0, tm // slab, norm_slab, 0)
        if has_extra:
            ox_ref[...] = mm(a_scr[...], wx_ref[...])

    o_ref[...] = mm(a_scr[...], w_ref[...])


def _norm_proj(a, g, w, w_extra=None, w_transposed=False, name="norm_proj"):
    m, k = a.shape
    n = w.shape[0] if w_transposed else w.shape[1]
    tm = min(m, 1024)
    tn = NORM_PROJ_TN
    w_spec = (pl.BlockSpec((tn, k), lambda i, j: (j, 0)) if w_transposed
              else pl.BlockSpec((k, tn), lambda i, j: (0, j)))
    in_specs = [pl.BlockSpec((tm, k), lambda i, j: (i, 0)), _const_spec((1, k)), w_spec]
    out_specs = [pl.BlockSpec((tm, tn), lambda i, j: (i, j))]
    out_shape = [jax.ShapeDtypeStruct((m, n), F32)]
    args = [a, g.reshape(1, k), w]
    if w_extra is not None:
        nx = w_extra.shape[0] if w_transposed else w_extra.shape[1]
        in_specs.append(_const_spec(w_extra.shape))
        out_specs.append(pl.BlockSpec((tm, nx), lambda i, j: (i, 0)))
        out_shape.append(jax.ShapeDtypeStruct((m, nx), F32))
        args.append(w_extra)
    return pl.pallas_call(
        functools.partial(_norm_proj_kernel, has_extra=w_extra is not None,
                          w_transposed=w_transposed),
        grid=(m // tm, n // tn),
        in_specs=in_specs, out_specs=out_specs, out_shape=out_shape,
        scratch_shapes=[pltpu.VMEM((tm, k), BF16)],
        compiler_params=_params(2),
        name=name,
    )(*args)


def _res_proj_kernel(a_ref, w_ref, r_ref, o_ref):
    o_ref[...] = jnp.dot(a_ref[...], w_ref[...], preferred_element_type=F32) + r_ref[...]


def _res_proj(a, w, res, name="res_proj"):
    m, k = a.shape
    n = w.shape[1]
    tm = min(m, 1024)
    tn = 1024
    return pl.pallas_call(
        _res_proj_kernel,
        grid=(m // tm, n // tn),
        in_specs=[pl.BlockSpec((tm, k), lambda i, j: (i, 0)),
                  pl.BlockSpec((k, tn), lambda i, j: (0, j)),
                  pl.BlockSpec((tm, tn), lambda i, j: (i, j))],
        out_specs=pl.BlockSpec((tm, tn), lambda i, j: (i, j)),
        out_shape=jax.ShapeDtypeStruct((m, n), F32),
        compiler_params=_params(2),
        name=name,
    )(a, w, res)


def _res_proj_norm_kernel(a_ref, w_ref, r_ref, g_ref, o_ref):
    tm = a_ref.shape[0]
    slab = min(tm, CHUNK)
    for s in range(tm // slab):
        rows = slice(s * slab, (s + 1) * slab)
        x = jnp.dot(a_ref[rows, :], w_ref[...], preferred_element_type=F32) + r_ref[rows, :]
        y = x * lax.rsqrt(jnp.mean(x * x, axis=-1, keepdims=True) + NORM_EPS)
        o_ref[rows, :] = y * g_ref[...]


def _res_proj_norm(a, w, res, g, name="res_proj_norm"):
    m, k = a.shape
    n = w.shape[1]
    tm = min(m, 512)
    return pl.pallas_call(
        _res_proj_norm_kernel,
        grid=(m // tm,),
        in_specs=[pl.BlockSpec((tm, k), lambda i: (i, 0)), _const_spec((k, n)),
                  pl.BlockSpec((tm, n), lambda i: (i, 0)), _const_spec((1, n))],
        out_specs=pl.BlockSpec((tm, n), lambda i: (i, 0)),
        out_shape=jax.ShapeDtypeStruct((m, n), F32),
        compiler_params=_params(1),
        name=name,
    )(a, w, res, g.reshape(1, n))


def _s5_kernel(*refs, pad_rows, n_sub):
    _for_subchunks(functools.partial(_s5_chunk, pad_rows=pad_rows), refs, (0, 1, 12), n_sub)


def _s5_chunk(u_ref, z_ref, wb_ref, wc_ref, apr_ref, api_ref, d_ref, gw_ref, gb_ref,
              sr0_ref, si0_ref, yin_ref, y_ref, sr_ref, si_ref, out_scr, *, pad_rows, is_first):
    del yin_ref
    c = u_ref.shape[0]
    seg = c // SUBLANES
    n_tiles = apr_ref.shape[0]
    tiles_per_oct = n_tiles // S5_OCT
    half = tiles_per_oct * LANES

    @pl.when(is_first)
    def _():
        sr_ref[...] = sr0_ref[...]
        si_ref[...] = si0_ref[...]

    n_slabs = out_scr.shape[0]

    def load_permuted(ref):
        for k in range(n_slabs):
            out_scr[k] = ref[:, k * LANES:(k + 1) * LANES]
        return jnp.concatenate(
            [jnp.concatenate([out_scr[k, pl.ds(i, SUBLANES, stride=seg), :] for i in range(seg)], axis=0)
             for k in range(n_slabs)], axis=1)

    u = load_permuted(u_ref)
    if pad_rows:
        row = lax.broadcasted_iota(jnp.int32, (c, 1), 0)
        t_of_row = (row & (SUBLANES - 1)) * seg + (row >> 3)
        u = jnp.where(t_of_row >= pad_rows, u, 0.0)

    zeros8 = jnp.zeros((SUBLANES, LANES), F32)

    def cmul_add(xr, xi, pr, pi, yr, yi):
        return xr + pr * yr - pi * yi, xi + pr * yi + pi * yr

    def scan_tile(j, bur, bui):
        blk = lambda i: slice(i * SUBLANES, (i + 1) * SUBLANES)
        a1r, a1i = apr_ref[j, 0:1, :], api_ref[j, 0:1, :]
        br, bi = bur[blk(0)], bui[blk(0)]
        loc = [(br, bi)]
        for i in range(1, seg):
            br, bi = cmul_add(bur[blk(i)], bui[blk(i)], a1r, a1i, br, bi)
            loc.append((br, bi))
        fr, fi = br, bi
        for d in (1, 2, 4):
            p = seg * d - 1
            fr, fi = cmul_add(fr, fi, apr_ref[j, p:p + 1, :], api_ref[j, p:p + 1, :],
                              _shift_rows(fr, d, zeros8), _shift_rows(fi, d, zeros8))
        cr, ci = sr_ref[j], si_ref[j]
        fr, fi = cmul_add(fr, fi, apr_ref[j, pl.ds(seg - 1, SUBLANES, stride=seg), :],
                          api_ref[j, pl.ds(seg - 1, SUBLANES, stride=seg), :], cr, ci)
        gr = _shift_rows(fr, 1, jnp.broadcast_to(cr, (SUBLANES, LANES)))
        gi = _shift_rows(fi, 1, jnp.broadcast_to(ci, (SUBLANES, LANES)))
        xs = [cmul_add(loc[i][0], loc[i][1], apr_ref[j, i:i + 1, :], api_ref[j, i:i + 1, :], gr, gi)
              for i in range(seg)]
        sr_ref[j] = fr[SUBLANES - 1:SUBLANES]
        si_ref[j] = fi[SUBLANES - 1:SUBLANES]
        return (jnp.concatenate([x[0] for x in xs], axis=0),
                jnp.concatenate([x[1] for x in xs], axis=0))

    ys = []
    for k in range(S5_OCT):
        bu = _dot(u[:, k * LANES:(k + 1) * LANES], wb_ref[k])
        xrs, xis = [], []
        for q in range(tiles_per_oct):
            xr, xi = scan_tile(k * tiles_per_oct + q, bu[:, q * LANES:(q + 1) * LANES],
                               bu[:, half + q * LANES:half + (q + 1) * LANES])
            xrs.append(xr)
            xis.append(xi)
        ys.append(_dot(jnp.concatenate(xrs + xis, axis=1), wc_ref[k]))
    y = jnp.concatenate(ys, axis=1) + d_ref[...] * u
    g = jax.nn.gelu(y)
    gate = _sigmoid(_dot(g, gw_ref[...]) + gb_ref[...])
    res = g * gate
    for k in range(n_slabs):
        cols = slice(k * LANES, (k + 1) * LANES)
        for i in range(seg):
            out_scr[k, pl.ds(i, SUBLANES, stride=seg), :] = res[i * SUBLANES:(i + 1) * SUBLANES, cols]
        y_ref[:, cols] = (out_scr[k] * _silu_of_half(z_ref[:, cols])).astype(y_ref.dtype)


def _s5_call(p, col_u, col_z, y_buf, col_y, wts, state0, bsz, n_chunks, pad_rows):
    c = CHUNK
    width = wts["d"].shape[1]
    n_tiles = wts["apr"].shape[0]
    n_sub, steps, row = _step_layout(n_chunks)
    st_spec_in = pl.BlockSpec((None, n_tiles, 1, LANES), lambda b, i: (0, 0, 0, 0),
                              pipeline_mode=pl.Buffered(1))
    st_spec_out = pl.BlockSpec((None, n_tiles, 1, LANES), lambda b, i: (b, 0, 0, 0))
    st_shape = jax.ShapeDtypeStruct((bsz, n_tiles, 1, LANES), F32)
    names = ["wb", "wc", "apr", "api", "d", "gw", "gb"]
    return pl.pallas_call(
        functools.partial(_s5_kernel, pad_rows=pad_rows, n_sub=n_sub),
        grid=(bsz, steps),
        in_specs=[pl.BlockSpec((c * n_sub, width), lambda b, i: (row(b, i), col_u)),
                  pl.BlockSpec((c * n_sub, width), lambda b, i: (row(b, i), col_z))]
        + [_const_spec(wts[n].shape) for n in names]
        + [st_spec_in, st_spec_in, pl.BlockSpec(memory_space=pl.ANY)],
        out_specs=[pl.BlockSpec((c * n_sub, width), lambda b, i: (row(b, i), col_y)),
                   st_spec_out, st_spec_out],
        out_shape=[jax.ShapeDtypeStruct(y_buf.shape, y_buf.dtype), st_shape, st_shape],
        scratch_shapes=[pltpu.VMEM((width // LANES, c, LANES), F32)],
        input_output_aliases={len(names) + 4: 0},
        compiler_params=_params(2),
        name="s5_mixer",
    )(p, p, *[wts[n] for n in names], state0[0], state0[1], y_buf)


def _s5_weights(lam_re, lam_im, log_dt, b_re, b_im, c_re, c_im, d, glu_w, glu_b):
    g_n, p_n = lam_re.shape
    dt = jnp.exp(log_dt)[:, None]
    mag = jnp.exp(lam_re * dt)
    ar, ai = mag * jnp.cos(lam_im * dt), mag * jnp.sin(lam_im * dt)
    den = lam_re * lam_re + lam_im * lam_im
    qr = ((ar - 1.0) * lam_re + ai * lam_im) / den
    qi = (ai * lam_re - (ar - 1.0) * lam_im) / den
    bbr = qr[..., None] * b_re - qi[..., None] * b_im
    bbi = qr[..., None] * b_im + qi[..., None] * b_re
    n_oct = g_n // S5_OCT
    eye = jnp.eye(S5_OCT, dtype=F32)

    def in_proj(w):
        w = w.reshape(n_oct, S5_OCT, p_n, S5_GROUP_SIZE)
        return jnp.einsum("kgph,gG->kghGp", w, eye).reshape(n_oct, S5_OCT * S5_GROUP_SIZE, S5_OCT * p_n)

    def out_proj(w):
        w = w.reshape(n_oct, S5_OCT, S5_GROUP_SIZE, p_n)
        return jnp.einsum("kghp,gG->kgpGh", w, eye).reshape(n_oct, S5_OCT * p_n, S5_OCT * S5_GROUP_SIZE)

    wb = jnp.concatenate([in_proj(bbr), in_proj(bbi)], axis=2).astype(BF16)
    wc = jnp.concatenate([out_proj(c_re), out_proj(-c_im)], axis=1).astype(BF16)
    pr, pi = ar.reshape(1, -1), ai.reshape(1, -1)
    while pr.shape[0] < CHUNK:
        lr_, li_ = pr[-1:], pi[-1:]
        pr, pi = (jnp.concatenate([pr, pr * lr_ - pi * li_], axis=0),
                  jnp.concatenate([pi, pr * li_ + pi * lr_], axis=0))
    n_tiles = g_n * p_n // LANES
    tile = lambda a: a.reshape(CHUNK, n_tiles, LANES).transpose(1, 0, 2)
    width = d.shape[0]
    return dict(wb=wb, wc=wc, apr=tile(pr), api=tile(pi), d=d.reshape(1, width),
                gw=glu_w.astype(BF16), gb=glu_b.reshape(1, width))


def _log_sigmoid(x):
    return jnp.minimum(x, 0.0) - jnp.log1p(jnp.exp(-jnp.abs(x)))


def _cummax_rows(x):
    c = x.shape[0]
    neg8 = jnp.full((SUBLANES, x.shape[1]), -jnp.inf, F32)
    d = 1
    while d < c:
        if d < SUBLANES:
            sh = _shift_rows(x, d, neg8)
        else:
            sh = jnp.concatenate([jnp.full((d, x.shape[1]), -jnp.inf, F32), x[:c - d]], axis=0)
        x = jnp.maximum(x, sh)
        d *= 2
    return x


def _mlstm_kernel(*refs, n_in_slabs, pad_rows, emit_state, n_sub):
    row_idx = tuple(range(2 * n_in_slabs)) + (2 * n_in_slabs + 11,)
    first_step = pl.program_id(1) == 0
    chunks = []
    for s in range(n_sub):
        rows = pl.ds(s * CHUNK, CHUNK)
        sub = [r.at[rows, :] if k in row_idx else r for k, r in enumerate(refs)]
        chunks.append(_mlstm_chunk(*sub, n_in_slabs=n_in_slabs, pad_rows=pad_rows, emit_state=emit_state,
                                   slot=s, is_first=first_step if s == 0 else None))
    for chunk in chunks:
        next(chunk)
    for chunk in chunks:
        for _ in chunk:
            pass


def _mlstm_chunk(*refs, n_in_slabs, pad_rows, emit_state, slot, is_first):
    xb_refs = refs[:n_in_slabs]
    zb_refs = refs[n_in_slabs:2 * n_in_slabs]
    k = 2 * n_in_slabs
    (cw_ref, cb_ref, wq_ref, wkt_ref, wv_ref, bg_ref, nw_ref, sk_ref,
     s0_ref, m0_ref, t0_ref, y_ref) = refs[k:k + 12]
    if emit_state:
        s_st, m_st, t_st, q_scr, kt_scr, v_scr, xc_scr = refs[k + 12:]
    else:
        q_scr, kt_scr, v_scr, xc_scr, s_st, m_st, t_st = refs[k + 12:]
    q_scr, kt_scr, v_scr, xc_scr = (r.at[slot] for r in (q_scr, kt_scr, v_scr, xc_scr))
    c, slab_w = xb_refs[0].shape
    n_heads, dh = s_st.shape[0], s_st.shape[1]
    n_blk = wq_ref.shape[0]
    y_off = y_ref.shape[1] - n_heads * dh
    scale = dh ** -0.5

    def slab_cols(slab_refs, lo, hi):
        parts = []
        while lo < hi:
            end = min(hi, (lo // slab_w + 1) * slab_w)
            parts.append(slab_refs[lo // slab_w][:, lo % slab_w:lo % slab_w + (end - lo)])
            lo = end
        return parts[0] if len(parts) == 1 else jnp.concatenate(parts, axis=1)

    if is_first is not None:
        @pl.when(is_first)
        def _():
            s_st[...] = s0_ref[...]
            m_st[...] = m0_ref[...]
            t_st[...] = t0_ref[...]

    row_c = lax.broadcasted_iota(jnp.int32, (c, 1), 0)
    lane = lax.broadcasted_iota(jnp.int32, (1, LANES), 1)
    y_ref[:, :y_off] = jnp.zeros((c, y_off), y_ref.dtype)

    gates = jnp.zeros((c, 2 * LANES), F32) + bg_ref[...]
    for j in range(n_blk):
        sl = slice(j * QKV_TILE, (j + 1) * QKV_TILE)
        xf = slab_cols(xb_refs, j * QKV_TILE, (j + 1) * QKV_TILE)
        if pad_rows:
            xf = jnp.where(row_c >= pad_rows, xf, 0.0)
        xc = _causal_conv_silu(xf, t_st[:, sl], cw_ref[:, sl], cb_ref[:, sl])
        t_st[:, sl] = xf[c - SUBLANES:]
        xcb = xc.astype(BF16)
        qg = jnp.dot(xcb, wq_ref[j], preferred_element_type=F32)
        vg = jnp.dot(xf.astype(BF16), wv_ref[j], preferred_element_type=F32)
        q_scr[:, sl] = qg[:, :QKV_TILE] * scale
        kt_scr[sl, :] = _dot_nt(wkt_ref[j], xcb)
        v_scr[:, sl] = vg[:, :QKV_TILE].astype(BF16)
        xc_scr[:, sl] = xc
        gates = gates + qg[:, QKV_TILE:] + vg[:, QKV_TILE:]

    ig = gates[:, :LANES]
    lf = _log_sigmoid(gates[:, LANES:])
    if pad_rows:
        ig = jnp.where(row_c >= pad_rows, ig, -jnp.inf)
        lf = jnp.where(row_c >= pad_rows, lf, 0.0)
    causal = _causal_masks(c)
    bcum = _dot_f32(causal.astype(F32), lf)
    a = ig - bcum
    m_old = m_st[...]
    big_m = jnp.maximum(_cummax_rows(a), m_old)
    w_prev = jnp.exp(m_old - big_m)
    e_inv = jnp.exp(-(bcum + big_m))
    m_last = big_m[c - 1:c]
    decay = jnp.exp(m_old - m_last)
    e_col = jnp.exp(a - m_last)
    m_st[...] = bcum[c - 1:c] + m_last
    a2_t = (a * LOG2E).T
    m2 = big_m * LOG2E
    e_t = e_col.T
    yield

    for h in range(n_heads):
        hs = slice(h * dh, (h + 1) * dh)
        wt = jnp.exp2(jnp.where(causal, a2_t[h:h + 1, :] - m2[:, h:h + 1], -jnp.inf))
        kt = kt_scr[hs, :]
        qh = q_scr[:, hs]
        s = _dot(qh, kt) * wt
        lhs = jnp.concatenate([s.astype(BF16), (qh * w_prev[:, h:h + 1]).astype(BF16)], axis=1)
        ones_h = jnp.broadcast_to(jnp.where(lane == h, 1.0, 0.0).astype(BF16), (c, LANES))
        vext = jnp.concatenate([v_scr[:, hs], ones_h], axis=1)
        st = s_st[h]
        nd = jnp.dot(lhs, jnp.concatenate([vext, st.astype(BF16)], axis=0),
                     preferred_element_type=F32)
        r_blk = 1.0 / jnp.maximum(jnp.abs(nd[:, dh:]), e_inv)
        hh = nd[:, :dh] * r_blk[:, h:h + 1]
        wkt = (kt * e_t[h:h + 1, :]).astype(BF16)
        s_st[h] = decay[:, h:h + 1] * st + jnp.dot(wkt, vext, preferred_element_type=F32)
        mu = jnp.mean(hh, axis=1, keepdims=True)
        hc = hh - mu
        var = jnp.mean(hc * hc, axis=1, keepdims=True)
        hn = hc * lax.rsqrt(var + HEAD_NORM_EPS) * nw_ref[:, hs]
        out = (hn + sk_ref[:, hs] * xc_scr[:, hs]) * _silu_of_half(slab_cols(zb_refs, h * dh, (h + 1) * dh))
        y_ref[:, y_off + h * dh:y_off + (h + 1) * dh] = out.astype(y_ref.dtype)


def _state_specs(shapes, emit_state):
    st_in = [pl.BlockSpec((None,) + s, lambda b, i, nd=len(s): (0,) * (nd + 1),
                          pipeline_mode=pl.Buffered(1)) for s in shapes]
    if emit_state:
        st_out = [pl.BlockSpec((None,) + s, lambda b, i, nd=len(s): (b,) + (0,) * nd) for s in shapes]
        return st_in, st_out, []
    return st_in, [], [pltpu.VMEM(s, F32) for s in shapes]


def _mlstm_call(p, col_x, col_z, slab_w, out_width, wts, state0, bsz, n_chunks, pad_rows, emit_state):
    c = CHUNK
    width = wts["cw"].shape[1]
    nh = MLSTM_HEADS
    dh = width // nh
    n_slabs = width // slab_w
    n_sub, steps, row = _step_layout(n_chunks)
    slab_spec = lambda col: pl.BlockSpec((c * n_sub, slab_w), lambda b, i: (row(b, i), col))
    shapes = [(nh, dh, dh + LANES), (1, LANES), (SUBLANES, width)]
    st_in, st_out, st_scr = _state_specs(shapes, emit_state)
    names = ["cw", "cb", "wq", "wkt", "wv", "bg", "nw", "sk"]
    work = [pltpu.VMEM((n_sub, c, width), F32), pltpu.VMEM((n_sub, width, c), F32),
            pltpu.VMEM((n_sub, c, width), BF16), pltpu.VMEM((n_sub, c, width), F32)]
    return pl.pallas_call(
        functools.partial(_mlstm_kernel, n_in_slabs=n_slabs, pad_rows=pad_rows, emit_state=emit_state,
                          n_sub=n_sub),
        grid=(bsz, steps),
        in_specs=[slab_spec(col_x + s) for s in range(n_slabs)]
        + [slab_spec(col_z + s) for s in range(n_slabs)]
        + [_const_spec(wts[n].shape) for n in names] + st_in,
        out_specs=[pl.BlockSpec((c * n_sub, out_width), lambda b, i: (row(b, i), 0))] + st_out,
        out_shape=[jax.ShapeDtypeStruct((bsz * n_chunks * c, out_width), BF16)]
        + ([jax.ShapeDtypeStruct((bsz,) + s, F32) for s in shapes] if emit_state else []),
        scratch_shapes=work + st_scr,
        compiler_params=_params(2),
        name="mlstm_mixer",
    )(*([p] * (2 * n_slabs)), *[wts[n] for n in names], *state0)


def _block_diag_tiles(w, tile):
    nb, b, _ = w.shape
    rows = w.reshape(nb * b // tile, tile, b)
    col = jnp.arange(tile)
    spread = (col[None, :] % b == jnp.arange(b)[:, None]).astype(w.dtype)
    same_block = col[:, None] // b == col[None, :] // b
    return jnp.where(same_block, jnp.einsum("jrb,bc->jrc", rows, spread), 0.0)


def _mlstm_weights(conv_w, conv_b, wq, wk, wv, w_gate, b_gate, norm_w, skip):
    width = conv_w.shape[1]
    nh = MLSTM_HEADS
    n_blk = width // QKV_TILE

    def gate_cols(w_hw, w_g):
        g = jnp.einsum("nio,nog->nig", w_hw, w_g.reshape(-1, QKV_BLOCK, 2 * nh)).reshape(width, 2 * nh)
        fill = jnp.zeros((width, LANES - nh), F32)
        out = jnp.concatenate([g[:, :nh], fill, g[:, nh:], fill], axis=1)
        return out.reshape(n_blk, QKV_TILE, 2 * LANES)

    g_qk = gate_cols(wq, w_gate[:width]) + gate_cols(wk, w_gate[width:2 * width])
    g_v = gate_cols(wv, w_gate[2 * width:])
    wq_t = jnp.concatenate([_block_diag_tiles(wq, QKV_TILE), g_qk], axis=2).astype(BF16)
    wv_t = jnp.concatenate([_block_diag_tiles(wv, QKV_TILE), g_v], axis=2).astype(BF16)
    wk_t = _block_diag_tiles(wk.transpose(0, 2, 1), QKV_TILE).astype(BF16)
    bg = jnp.zeros((1, 2 * LANES), F32)
    bg = bg.at[0, :nh].set(b_gate[:nh]).at[0, LANES:LANES + nh].set(b_gate[nh:])
    return dict(cw=0.5 * conv_w, cb=0.5 * conv_b.reshape(1, width), wq=wq_t, wkt=wk_t, wv=wv_t, bg=bg,
                nw=norm_w.reshape(1, width), sk=skip.reshape(1, width))


def _ssd_kernel(*refs, pad_rows, emit_state, n_sub):
    _for_subchunks(functools.partial(_ssd_chunk, pad_rows=pad_rows, emit_state=emit_state), refs,
                   (0, 1, 2, 3, 4, 19), n_sub)


def _ssd_chunk(*refs, pad_rows, emit_state, is_first):
    (z_ref, xs_ref, bm_ref, cm_ref, dt_ref, cwx_ref, cbx_ref, cwb_ref, cbb_ref, cwc_ref, cbc_ref,
     dtb_ref, a_ref, d_ref, gn_ref, s0_ref, tx0_ref, tb0_ref, tc0_ref, y_ref) = refs[:20]
    if emit_state:
        s_st, tx_st, tb_st, tc_st, xs_scr, bm_scr, cm_scr, y_scr = refs[20:]
    else:
        xs_scr, bm_scr, cm_scr, y_scr, s_st, tx_st, tb_st, tc_st = refs[20:]
    c, inner = xs_ref.shape
    n_pairs = s_st.shape[0]
    pairs_per_group = n_pairs // SSD_GROUPS
    gw = inner // SSD_GROUPS

    @pl.when(is_first)
    def _():
        s_st[...] = s0_ref[...]
        tx_st[...] = tx0_ref[...]
        tb_st[...] = tb0_ref[...]
        tc_st[...] = tc0_ref[...]

    row_c = lax.broadcasted_iota(jnp.int32, (c, 1), 0)

    def conv_into(dst_scr, src_ref, tail_st, w_ref, b_ref):
        for j in range(src_ref.shape[1] // CONV_BLOCK):
            sl = slice(j * CONV_BLOCK, (j + 1) * CONV_BLOCK)
            x = src_ref[:, sl]
            if pad_rows:
                x = jnp.where(row_c >= pad_rows, x, 0.0)
            dst_scr[:, sl] = _causal_conv_silu(x, tail_st[:, sl], w_ref[:, sl], b_ref[:, sl])
            tail_st[:, sl] = x[c - SUBLANES:]

    conv_into(xs_scr, xs_ref, tx_st, cwx_ref, cbx_ref)
    conv_into(bm_scr, bm_ref, tb_st, cwb_ref, cbb_ref)
    conv_into(cm_scr, cm_ref, tc_st, cwc_ref, cbc_ref)

    dt = jax.nn.softplus(dt_ref[...] + dtb_ref[...])
    if pad_rows:
        dt = jnp.where(row_c >= pad_rows, dt, 0.0)
    causal = _causal_masks(c)
    cum = _dot_f32(causal.astype(F32), dt * a_ref[...])
    cum_t = cum.T
    dt_t = dt.T
    dec = jnp.exp(cum[c - 1:c, :] - cum) * dt
    src2_t = (cum_t - jnp.log(dt_t)) * LOG2E
    cum2 = cum * LOG2E
    e_last = jnp.exp(cum[c - 1:c, :])
    lo = lax.broadcasted_iota(jnp.int32, (1, LANES), 1) < SSD_HEAD_DIM

    for g in range(SSD_GROUPS):
        gs = slice(g * SSD_STATE, (g + 1) * SSD_STATE)
        cg = cm_scr[:, gs]
        bg_tb = bm_scr[:, gs].T.astype(BF16)
        cb = jnp.dot(cg.astype(BF16), bg_tb, preferred_element_type=F32)
        for pr in range(pairs_per_group):
            pidx = g * pairs_per_group + pr
            ps = slice(pidx * LANES, (pidx + 1) * LANES)
            xp = xs_scr[:, ps]
            xpb = xp.astype(BF16)
            st = s_st[pidx]
            rhs = jnp.concatenate([xpb, st.astype(BF16)], axis=0)
            ys = []
            for h in (2 * pidx, 2 * pidx + 1):
                bc = jnp.broadcast_to(cum2[:, h:h + 1], (c, c))
                seg = jnp.exp2(jnp.where(causal, bc - src2_t[h:h + 1, :], -jnp.inf))
                lhs = jnp.concatenate([(cb * seg).astype(BF16), (cg * jnp.exp2(bc)).astype(BF16)],
                                      axis=1)
                ys.append(jnp.dot(lhs, rhs, preferred_element_type=F32))
            h0 = 2 * pidx
            y_scr[:, ps] = jnp.where(lo, ys[0], ys[1]) + d_ref[:, ps] * xp
            el = jnp.where(lo, e_last[:, h0:h0 + 1], e_last[:, h0 + 1:h0 + 2])
            xdec = xp * jnp.where(lo, dec[:, h0:h0 + 1], dec[:, h0 + 1:h0 + 2])
            s_st[pidx] = el * st + jnp.dot(bg_tb, xdec.astype(BF16), preferred_element_type=F32)

    for g in range(SSD_GROUPS):
        gs = slice(g * gw, (g + 1) * gw)
        yg = y_scr[:, gs] * _silu_of_half(z_ref[:, gs])
        yg = yg * lax.rsqrt(jnp.mean(yg * yg, axis=1, keepdims=True) + NORM_EPS)
        y_ref[:, gs] = (yg * gn_ref[:, gs]).astype(y_ref.dtype)


def _ssd_call(p, pdt, wts, state0, bsz, n_chunks, pad_rows, emit_state):
    c = CHUNK
    inner = wts["d"].shape[1]
    gn = wts["cwb"].shape[1]
    n_pairs = inner // LANES
    n_sub, steps, row = _step_layout(n_chunks)
    cs = c * n_sub
    shapes = [(n_pairs, SSD_STATE, LANES), (SUBLANES, inner), (SUBLANES, gn), (SUBLANES, gn)]
    st_in, st_out, st_scr = _state_specs(shapes, emit_state)
    names = ["cwx", "cbx", "cwb", "cbb", "cwc", "cbc", "dtb", "a", "d", "gn"]
    bcol = 2 * inner // gn
    work = [pltpu.VMEM((c, inner), F32), pltpu.VMEM((c, gn), F32), pltpu.VMEM((c, gn), F32),
            pltpu.VMEM((c, inner), F32)]
    return pl.pallas_call(
        functools.partial(_ssd_kernel, pad_rows=pad_rows, emit_state=emit_state, n_sub=n_sub),
        grid=(bsz, steps),
        in_specs=[pl.BlockSpec((cs, inner), lambda b, i: (row(b, i), 0)),
                  pl.BlockSpec((cs, inner), lambda b, i: (row(b, i), 1)),
                  pl.BlockSpec((cs, gn), lambda b, i: (row(b, i), bcol)),
                  pl.BlockSpec((cs, gn), lambda b, i: (row(b, i), bcol + 1)),
                  pl.BlockSpec((cs, LANES), lambda b, i: (row(b, i), 0))]
        + [_const_spec(wts[n].shape) for n in names] + st_in,
        out_specs=[pl.BlockSpec((cs, inner), lambda b, i: (row(b, i), 0))] + st_out,
        out_shape=[jax.ShapeDtypeStruct((bsz * n_chunks * c, inner), BF16)]
        + ([jax.ShapeDtypeStruct((bsz,) + s, F32) for s in shapes] if emit_state else []),
        scratch_shapes=work + st_scr,
        compiler_params=_params(2),
        name="ssd_mixer",
    )(p, p, p, p, pdt, *[wts[n] for n in names], *state0)


def _ssd_weights(conv_w, conv_b, dt_bias, a_log, d, gnorm):
    n_heads = dt_bias.shape[0]
    inner = gnorm.shape[0]
    gn = (conv_w.shape[1] - inner) // 2
    pad = lambda v: jnp.zeros((1, LANES), F32).at[0, :n_heads].set(v)
    cb = 0.5 * conv_b.reshape(1, -1)
    conv_w = 0.5 * conv_w
    return dict(cwx=conv_w[:, :inner], cbx=cb[:, :inner],
                cwb=conv_w[:, inner:inner + gn], cbb=cb[:, inner:inner + gn],
                cwc=conv_w[:, inner + gn:], cbc=cb[:, inner + gn:],
                dtb=pad(dt_bias), a=pad(-jnp.exp(a_log)),
                d=jnp.repeat(d, inner // n_heads).reshape(1, inner), gn=gnorm.reshape(1, inner))


def kernel(x, meta_tokens, ab_norm, ab_w_in, s5_lambda_re, s5_lambda_im, s5_log_dt, s5_b_re, s5_b_im, s5_c_re, s5_c_im, s5_d, s5_glu_w, s5_glu_b, ml_conv_w, ml_conv_b, ml_wq, ml_wk, ml_wv, ml_w_gate, ml_b_gate, ml_norm, ml_skip, ab_w_out, ssd_norm, ssd_w_in, ssd_conv_w, ssd_conv_b, ssd_dt_bias, ssd_a_log, ssd_d, ssd_gnorm, ssd_w_out, final_norm):
    bsz, seq, d_model = x.shape
    assert seq % CHUNK == 0 and N_META <= CHUNK
    n_chunks = seq // CHUNK
    s5_w = s5_d.shape[1]
    ml_w = ml_conv_w.shape[2]
    ssd_inner = ssd_gnorm.shape[1]
    ssd_heads = ssd_dt_bias.shape[1]
    ssd_main = ssd_w_in.shape[2] - ssd_heads
    pad_rows = CHUNK - N_META

    half = lambda n: jnp.full((n,), 0.5, F32)
    one = lambda n: jnp.ones((n,), F32)
    w_in0 = (ab_w_in[0] * jnp.concatenate([one(s5_w), half(s5_w), one(ml_w), half(ml_w)])).astype(BF16)
    slab_w = s5_w
    w_out0 = ab_w_out[0].astype(BF16)
    s5_wts = _s5_weights(s5_lambda_re[0], s5_lambda_im[0], s5_log_dt[0], s5_b_re[0], s5_b_im[0],
                         s5_c_re[0], s5_c_im[0], s5_d[0], s5_glu_w[0], s5_glu_b[0])
    ml_wts = _mlstm_weights(ml_conv_w[0], ml_conv_b[0], ml_wq[0], ml_wk[0], ml_wv[0],
                            ml_w_gate[0], ml_b_gate[0], ml_norm[0], ml_skip[0])
    w_in1_t = jnp.swapaxes(ssd_w_in[0], 0, 1)
    w_in1_main = (w_in1_t[:ssd_main]
                  * jnp.concatenate([half(ssd_inner), one(ssd_main - ssd_inner)])[:, None]).astype(BF16)
    w_in1_dt = jnp.zeros((LANES, d_model), F32).at[:ssd_heads].set(w_in1_t[ssd_main:]).astype(BF16)
    w_out1 = ssd_w_out[0].astype(BF16)
    ssd_wts = _ssd_weights(ssd_conv_w[0], ssd_conv_b[0], ssd_dt_bias[0], ssd_a_log[0], ssd_d[0],
                           ssd_gnorm[0])

    n_tiles = s5_wts["apr"].shape[0]
    dh = ml_w // MLSTM_HEADS
    zeros = lambda *s: jnp.zeros(s, F32)
    s5_zero = (zeros(1, n_tiles, 1, LANES), zeros(1, n_tiles, 1, LANES))
    ml_zero = (zeros(1, MLSTM_HEADS, dh, dh + LANES), zeros(1, 1, LANES), zeros(1, SUBLANES, ml_w))
    gn = (ssd_main - 2 * ssd_inner) // 2
    ssd_zero = (zeros(1, ssd_inner // LANES, SSD_STATE, LANES), zeros(1, SUBLANES, ssd_inner),
                zeros(1, SUBLANES, gn), zeros(1, SUBLANES, gn))

    def layer0(h, b, nc, pad, s5_state, ml_state, emit_state):
        p = _norm_proj(h, ab_norm[0], w_in0, name="ab_in_proj")[0]
        y, *ml_out = _mlstm_call(p, 2 * s5_w // slab_w, (2 * s5_w + ml_w) // slab_w, slab_w,
                                 s5_w + ml_w, ml_wts, ml_state, b, nc, pad, emit_state)
        y, sr, si = _s5_call(p, 0, 1, y, 0, s5_wts, s5_state, b, nc, pad)
        h = _res_proj(y, w_out0, h, name="ab_out_proj")
        return h, (sr, si), tuple(ml_out)

    def layer1_mixer(h, b, nc, pad, ssd_state, emit_state):
        p, pdt = _norm_proj(h, ssd_norm[0], w_in1_main, w_in1_dt, w_transposed=True,
                            name="ssd_in_proj")
        y, *st = _ssd_call(p, pdt, ssd_wts, ssd_state, b, nc, pad, emit_state)
        return y, tuple(st)

    hm = jnp.concatenate([zeros(pad_rows, d_model), meta_tokens.astype(F32)], axis=0)
    hm, s5_state, ml_state = layer0(hm, 1, 1, pad_rows, s5_zero, ml_zero, True)
    _, ssd_state = layer1_mixer(hm, 1, 1, pad_rows, ssd_zero, True)

    h = x.reshape(bsz * seq, d_model)
    h, _, _ = layer0(h, bsz, n_chunks, 0, s5_state, ml_state, False)
    y, _ = layer1_mixer(h, bsz, n_chunks, 0, ssd_state, False)
    out = _res_proj_norm(y, w_out1, h, final_norm, name="ssd_out_proj_norm")
    return out.reshape(bsz, seq, d_model)
```

```python
import functools

import jax
import jax.numpy as jnp
from jax import lax
from jax.experimental import pallas as pl
from jax.experimental.pallas import tpu as pltpu

F32 = jnp.float32
BF16 = jnp.bfloat16

N_META = 16
CHUNK = 128
NORM_EPS = 1e-6
HEAD_NORM_EPS = 1e-5
LANES = 128
SUBLANES = 8
VMEM_LIMIT = 56 * 1024 * 1024

S5_GROUP_SIZE = 16
S5_STATE = 64
S5_OCT = 8
MLSTM_HEADS = 8
QKV_BLOCK = 4
QKV_TILE = 256
SSD_HEAD_DIM = 64
SSD_STATE = 128
SSD_GROUPS = 8
CONV_BLOCK = 512
LOG2E = 1.4426950408889634
NORM_PROJ_TN = 1024
SUBCHUNKS = 2


def _params(n_grid):
    return pltpu.CompilerParams(dimension_semantics=("arbitrary",) * n_grid,
                                vmem_limit_bytes=VMEM_LIMIT)


def _const_spec(shape):
    nd = len(shape)
    return pl.BlockSpec(shape, lambda *_: (0,) * nd, pipeline_mode=pl.Buffered(1))


def _dot(a, b):
    return jnp.dot(a.astype(BF16), b.astype(BF16), preferred_element_type=F32)


def _dot_nt(a, b):
    return lax.dot_general(a.astype(BF16), b.astype(BF16), (((1,), (1,)), ((), ())),
                           preferred_element_type=F32)


def _dot_f32(a, b):
    return jnp.dot(a, b, preferred_element_type=F32, precision=lax.Precision.HIGHEST)


def _sigmoid(x):
    return 0.5 * (1.0 + jnp.tanh(0.5 * x))


def _silu_of_half(h):
    return h + h * jnp.tanh(h)


def _shift_rows(x, d, prev8):
    rolled = pltpu.roll(x, d, axis=0)
    row8 = lax.broadcasted_iota(jnp.int32, (SUBLANES, x.shape[1]), 0)
    head = jnp.where(row8 < d, pltpu.roll(prev8, d, axis=0), rolled[:SUBLANES])
    if x.shape[0] == SUBLANES:
        return head
    return jnp.concatenate([head, rolled[SUBLANES:]], axis=0)


def _causal_conv_silu(x, tail8, w_half, b_half):
    assert w_half.shape[0] == 4
    w0, w1, w2, w3 = (w_half[i:i + 1, :] for i in range(4))
    x2 = _shift_rows(x, 2, tail8)
    u = x * w2 + x2 * w0
    u_tail = tail8 * w2 + pltpu.roll(tail8, 2, axis=0) * w0
    h = x * w3 + x2 * w1 + b_half + _shift_rows(u, 1, u_tail)
    return h + h * jnp.tanh(h)


def _causal_masks(c):
    r_i = lax.broadcasted_iota(jnp.int32, (c, c), 0)
    c_i = lax.broadcasted_iota(jnp.int32, (c, c), 1)
    return r_i >= c_i


def _pipeline_subchunks(chunk_gen, refs, row_idx, n_sub):
    first_step = pl.program_id(1) == 0
    chunks = []
    for s in range(n_sub):
        rows = pl.ds(s * CHUNK, CHUNK)
        sub = [r.at[rows, :] if k in row_idx else r for k, r in enumerate(refs)]
        chunks.append(chunk_gen(*sub, slot=s, is_first=first_step if s == 0 else None))
    for chunk in chunks:
        next(chunk)
    for chunk in chunks:
        for _ in chunk:
            pass


def _step_layout(n_chunks):
    n_sub = SUBCHUNKS if n_chunks % SUBCHUNKS == 0 else 1
    steps = n_chunks // n_sub
    return n_sub, steps, lambda b, i: b * steps + i


def _norm_proj_kernel(a_ref, g_ref, w_ref, *rest, has_extra, w_transposed):
    if has_extra:
        wx_ref, o_ref, ox_ref, a_scr = rest
    else:
        o_ref, a_scr = rest
    tm = a_ref.shape[0]
    slab = min(tm, CHUNK)

    def mm(lhs, rhs):
        if w_transposed:
            return lax.dot_general(lhs, rhs, (((1,), (1,)), ((), ())), preferred_element_type=F32)
        return jnp.dot(lhs, rhs, preferred_element_type=F32)

    @pl.when(pl.program_id(1) == 0)
    def _():
        def norm_slab(i, carry):
            rows = pl.ds(pl.multiple_of(i * slab, slab), slab)
            x = a_ref[rows, :]
            y = x * lax.rsqrt(jnp.mean(x * x, axis=-1, keepdims=True) + NORM_EPS)
            a_scr[rows, :] = (y * g_ref[...]).astype(BF16)
            return carry

        lax.fori_loop(0, tm // slab, norm_slab, 0)
        if has_extra:
            ox_ref[...] = mm(a_scr[...], wx_ref[...])

    o_ref[...] = mm(a_scr[...], w_ref[...])


def _norm_proj(a, g, w, w_extra=None, w_transposed=False, name="norm_proj"):
    m, k = a.shape
    n = w.shape[0] if w_transposed else w.shape[1]
    tm = min(m, 1024)
    tn = NORM_PROJ_TN
    w_spec = (pl.BlockSpec((tn, k), lambda i, j: (j, 0)) if w_transposed
              else pl.BlockSpec((k, tn), lambda i, j: (0, j)))
    in_specs = [pl.BlockSpec((tm, k), lambda i, j: (i, 0)), _const_spec((1, k)), w_spec]
    out_specs = [pl.BlockSpec((tm, tn), lambda i, j: (i, j))]
    out_shape = [jax.ShapeDtypeStruct((m, n), F32)]
    args = [a, g.reshape(1, k), w]
    if w_extra is not None:
        nx = w_extra.shape[0] if w_transposed else w_extra.shape[1]
        in_specs.append(_const_spec(w_extra.shape))
        out_specs.append(pl.BlockSpec((tm, nx), lambda i, j: (i, 0)))
        out_shape.append(jax.ShapeDtypeStruct((m, nx), F32))
        args.append(w_extra)
    return pl.pallas_call(
        functools.partial(_norm_proj_kernel, has_extra=w_extra is not None,
                          w_transposed=w_transposed),
        grid=(m // tm, n // tn),
        in_specs=in_specs, out_specs=out_specs, out_shape=out_shape,
        scratch_shapes=[pltpu.VMEM((tm, k), BF16)],
        compiler_params=_params(2),
        name=name,
    )(*args)


def _res_proj_kernel(a_ref, w_ref, r_ref, o_ref):
    o_ref[...] = jnp.dot(a_ref[...], w_ref[...], preferred_element_type=F32) + r_ref[...]


def _res_proj(a, w, res, name="res_proj"):
    m, k = a.shape
    n = w.shape[1]
    tm = min(m, 1024)
    tn = 1024
    return pl.pallas_call(
        _res_proj_kernel,
        grid=(m // tm, n // tn),
        in_specs=[pl.BlockSpec((tm, k), lambda i, j: (i, 0)),
                  pl.BlockSpec((k, tn), lambda i, j: (0, j)),
                  pl.BlockSpec((tm, tn), lambda i, j: (i, j))],
        out_specs=pl.BlockSpec((tm, tn), lambda i, j: (i, j)),
        out_shape=jax.ShapeDtypeStruct((m, n), F32),
        compiler_params=_params(2),
        name=name,
    )(a, w, res)


def _res_proj_norm_kernel(a_ref, w_ref, r_ref, g_ref, o_ref):
    tm = a_ref.shape[0]
    slab = min(tm, CHUNK)
    for s in range(tm // slab):
        rows = slice(s * slab, (s + 1) * slab)
        x = jnp.dot(a_ref[rows, :], w_ref[...], preferred_element_type=F32) + r_ref[rows, :]
        y = x * lax.rsqrt(jnp.mean(x * x, axis=-1, keepdims=True) + NORM_EPS)
        o_ref[rows, :] = y * g_ref[...]


def _res_proj_norm(a, w, res, g, name="res_proj_norm"):
    m, k = a.shape
    n = w.shape[1]
    tm = min(m, 512)
    return pl.pallas_call(
        _res_proj_norm_kernel,
        grid=(m // tm,),
        in_specs=[pl.BlockSpec((tm, k), lambda i: (i, 0)), _const_spec((k, n)),
                  pl.BlockSpec((tm, n), lambda i: (i, 0)), _const_spec((1, n))],
        out_specs=pl.BlockSpec((tm, n), lambda i: (i, 0)),
        out_shape=jax.ShapeDtypeStruct((m, n), F32),
        compiler_params=_params(1),
        name=name,
    )(a, w, res, g.reshape(1, n))


def _s5_kernel(*refs, pad_rows, n_sub):
    _pipeline_subchunks(functools.partial(_s5_chunk, pad_rows=pad_rows), refs, (0, 1, 12), n_sub)


def _s5_chunk(u_ref, z_ref, wb_ref, wc_ref, apr_ref, api_ref, d_ref, gw_ref, gb_ref,
              sr0_ref, si0_ref, yin_ref, y_ref, sr_ref, si_ref, out_scr, *, pad_rows, slot, is_first):
    del yin_ref
    c = u_ref.shape[0]
    seg = c // SUBLANES
    n_tiles = apr_ref.shape[0]
    tiles_per_oct = n_tiles // S5_OCT
    half = tiles_per_oct * LANES

    if is_first is not None:
        @pl.when(is_first)
        def _():
            sr_ref[...] = sr0_ref[...]
            si_ref[...] = si0_ref[...]

    out_scr = out_scr.at[slot]
    n_slabs = out_scr.shape[0]

    def load_permuted(ref):
        for k in range(n_slabs):
            out_scr[k] = ref[:, k * LANES:(k + 1) * LANES]
        return jnp.concatenate(
            [jnp.concatenate([out_scr[k, pl.ds(i, SUBLANES, stride=seg), :] for i in range(seg)], axis=0)
             for k in range(n_slabs)], axis=1)

    u = load_permuted(u_ref)
    if pad_rows:
        row = lax.broadcasted_iota(jnp.int32, (c, 1), 0)
        t_of_row = (row & (SUBLANES - 1)) * seg + (row >> 3)
        u = jnp.where(t_of_row >= pad_rows, u, 0.0)

    zeros8 = jnp.zeros((SUBLANES, LANES), F32)

    def cmul_add(xr, xi, pr, pi, yr, yi):
        return xr + pr * yr - pi * yi, xi + pr * yi + pi * yr

    def scan_tile(j, bur, bui):
        blk = lambda i: slice(i * SUBLANES, (i + 1) * SUBLANES)
        a1r, a1i = apr_ref[j, 0:1, :], api_ref[j, 0:1, :]
        br, bi = bur[blk(0)], bui[blk(0)]
        loc = [(br, bi)]
        for i in range(1, seg):
            br, bi = cmul_add(bur[blk(i)], bui[blk(i)], a1r, a1i, br, bi)
            loc.append((br, bi))
        fr, fi = br, bi
        for d in (1, 2, 4):
            p = seg * d - 1
            fr, fi = cmul_add(fr, fi, apr_ref[j, p:p + 1, :], api_ref[j, p:p + 1, :],
                              _shift_rows(fr, d, zeros8), _shift_rows(fi, d, zeros8))
        cr, ci = sr_ref[j], si_ref[j]
        fr, fi = cmul_add(fr, fi, apr_ref[j, pl.ds(seg - 1, SUBLANES, stride=seg), :],
                          api_ref[j, pl.ds(seg - 1, SUBLANES, stride=seg), :], cr, ci)
        gr = _shift_rows(fr, 1, jnp.broadcast_to(cr, (SUBLANES, LANES)))
        gi = _shift_rows(fi, 1, jnp.broadcast_to(ci, (SUBLANES, LANES)))
        xs = [cmul_add(loc[i][0], loc[i][1], apr_ref[j, i:i + 1, :], api_ref[j, i:i + 1, :], gr, gi)
              for i in range(seg)]
        sr_ref[j] = fr[SUBLANES - 1:SUBLANES]
        si_ref[j] = fi[SUBLANES - 1:SUBLANES]
        return (jnp.concatenate([x[0] for x in xs], axis=0),
                jnp.concatenate([x[1] for x in xs], axis=0))

    ys = []
    for k in range(S5_OCT):
        bu = _dot(u[:, k * LANES:(k + 1) * LANES], wb_ref[k])
        xrs, xis = [], []
        for q in range(tiles_per_oct):
            xr, xi = scan_tile(k * tiles_per_oct + q, bu[:, q * LANES:(q + 1) * LANES],
                               bu[:, half + q * LANES:half + (q + 1) * LANES])
            xrs.append(xr)
            xis.append(xi)
        ys.append(_dot(jnp.concatenate(xrs + xis, axis=1), wc_ref[k]))
    yield
    y = jnp.concatenate(ys, axis=1) + d_ref[...] * u
    g = jax.nn.gelu(y)
    gate = _sigmoid(_dot(g, gw_ref[...]) + gb_ref[...])
    res = g * gate
    for k in range(n_slabs):
        cols = slice(k * LANES, (k + 1) * LANES)
        for i in range(seg):
            out_scr[k, pl.ds(i, SUBLANES, stride=seg), :] = res[i * SUBLANES:(i + 1) * SUBLANES, cols]
        y_ref[:, cols] = (out_scr[k] * _silu_of_half(z_ref[:, cols])).astype(y_ref.dtype)


def _s5_call(p, col_u, col_z, y_buf, col_y, wts, state0, bsz, n_chunks, pad_rows):
    c = CHUNK
    width = wts["d"].shape[1]
    n_tiles = wts["apr"].shape[0]
    n_sub, steps, row = _step_layout(n_chunks)
    st_spec_in = pl.BlockSpec((None, n_tiles, 1, LANES), lambda b, i: (0, 0, 0, 0),
                              pipeline_mode=pl.Buffered(1))
    st_spec_out = pl.BlockSpec((None, n_tiles, 1, LANES), lambda b, i: (b, 0, 0, 0))
    st_shape = jax.ShapeDtypeStruct((bsz, n_tiles, 1, LANES), F32)
    names = ["wb", "wc", "apr", "api", "d", "gw", "gb"]
    return pl.pallas_call(
        functools.partial(_s5_kernel, pad_rows=pad_rows, n_sub=n_sub),
        grid=(bsz, steps),
        in_specs=[pl.BlockSpec((c * n_sub, width), lambda b, i: (row(b, i), col_u)),
                  pl.BlockSpec((c * n_sub, width), lambda b, i: (row(b, i), col_z))]
        + [_const_spec(wts[n].shape) for n in names]
        + [st_spec_in, st_spec_in, pl.BlockSpec(memory_space=pl.ANY)],
        out_specs=[pl.BlockSpec((c * n_sub, width), lambda b, i: (row(b, i), col_y)),
                   st_spec_out, st_spec_out],
        out_shape=[jax.ShapeDtypeStruct(y_buf.shape, y_buf.dtype), st_shape, st_shape],
        scratch_shapes=[pltpu.VMEM((n_sub, width // LANES, c, LANES), F32)],
        input_output_aliases={len(names) + 4: 0},
        compiler_params=_params(2),
        name="s5_mixer",
    )(p, p, *[wts[n] for n in names], state0[0], state0[1], y_buf)


def _s5_weights(lam_re, lam_im, log_dt, b_re, b_im, c_re, c_im, d, glu_w, glu_b):
    g_n, p_n = lam_re.shape
    dt = jnp.exp(log_dt)[:, None]
    mag = jnp.exp(lam_re * dt)
    ar, ai = mag * jnp.cos(lam_im * dt), mag * jnp.sin(lam_im * dt)
    den = lam_re * lam_re + lam_im * lam_im
    qr = ((ar - 1.0) * lam_re + ai * lam_im) / den
    qi = (ai * lam_re - (ar - 1.0) * lam_im) / den
    bbr = qr[..., None] * b_re - qi[..., None] * b_im
    bbi = qr[..., None] * b_im + qi[..., None] * b_re
    n_oct = g_n // S5_OCT
    eye = jnp.eye(S5_OCT, dtype=F32)

    def in_proj(w):
        w = w.reshape(n_oct, S5_OCT, p_n, S5_GROUP_SIZE)
        return jnp.einsum("kgph,gG->kghGp", w, eye).reshape(n_oct, S5_OCT * S5_GROUP_SIZE, S5_OCT * p_n)

    def out_proj(w):
        w = w.reshape(n_oct, S5_OCT, S5_GROUP_SIZE, p_n)
        return jnp.einsum("kghp,gG->kgpGh", w, eye).reshape(n_oct, S5_OCT * p_n, S5_OCT * S5_GROUP_SIZE)

    wb = jnp.concatenate([in_proj(bbr), in_proj(bbi)], axis=2).astype(BF16)
    wc = jnp.concatenate([out_proj(c_re), out_proj(-c_im)], axis=1).astype(BF16)
    pr, pi = ar.reshape(1, -1), ai.reshape(1, -1)
    while pr.shape[0] < CHUNK:
        lr_, li_ = pr[-1:], pi[-1:]
        pr, pi = (jnp.concatenate([pr, pr * lr_ - pi * li_], axis=0),
                  jnp.concatenate([pi, pr * li_ + pi * lr_], axis=0))
    n_tiles = g_n * p_n // LANES
    tile = lambda a: a.reshape(CHUNK, n_tiles, LANES).transpose(1, 0, 2)
    width = d.shape[0]
    return dict(wb=wb, wc=wc, apr=tile(pr), api=tile(pi), d=d.reshape(1, width),
                gw=glu_w.astype(BF16), gb=glu_b.reshape(1, width))


def _log_sigmoid(x):
    return jnp.minimum(x, 0.0) - jnp.log1p(jnp.exp(-jnp.abs(x)))


def _cummax_rows(x):
    c = x.shape[0]
    neg8 = jnp.full((SUBLANES, x.shape[1]), -jnp.inf, F32)
    d = 1
    while d < c:
        if d < SUBLANES:
            sh = _shift_rows(x, d, neg8)
        else:
            sh = jnp.concatenate([jnp.full((d, x.shape[1]), -jnp.inf, F32), x[:c - d]], axis=0)
        x = jnp.maximum(x, sh)
        d *= 2
    return x


def _mlstm_kernel(*refs, n_in_slabs, pad_rows, emit_state, n_sub):
    row_idx = tuple(range(2 * n_in_slabs)) + (2 * n_in_slabs + 11,)
    _pipeline_subchunks(functools.partial(_mlstm_chunk, n_in_slabs=n_in_slabs, pad_rows=pad_rows,
                                          emit_state=emit_state), refs, row_idx, n_sub)


def _mlstm_chunk(*refs, n_in_slabs, pad_rows, emit_state, slot, is_first):
    xb_refs = refs[:n_in_slabs]
    zb_refs = refs[n_in_slabs:2 * n_in_slabs]
    k = 2 * n_in_slabs
    (cw_ref, cb_ref, wq_ref, wkt_ref, wv_ref, bg_ref, nw_ref, sk_ref,
     s0_ref, m0_ref, t0_ref, y_ref) = refs[k:k + 12]
    if emit_state:
        s_st, m_st, t_st, q_scr, kt_scr, v_scr, xc_scr = refs[k + 12:]
    else:
        q_scr, kt_scr, v_scr, xc_scr, s_st, m_st, t_st = refs[k + 12:]
    q_scr, kt_scr, v_scr, xc_scr = (r.at[slot] for r in (q_scr, kt_scr, v_scr, xc_scr))
    c, slab_w = xb_refs[0].shape
    n_heads, dh = s_st.shape[0], s_st.shape[1]
    n_blk = wq_ref.shape[0]
    y_off = y_ref.shape[1] - n_heads * dh
    scale = dh ** -0.5

    def slab_cols(slab_refs, lo, hi):
        parts = []
        while lo < hi:
            end = min(hi, (lo // slab_w + 1) * slab_w)
            parts.append(slab_refs[lo // slab_w][:, lo % slab_w:lo % slab_w + (end - lo)])
            lo = end
        return parts[0] if len(parts) == 1 else jnp.concatenate(parts, axis=1)

    if is_first is not None:
        @pl.when(is_first)
        def _():
            s_st[...] = s0_ref[...]
            m_st[...] = m0_ref[...]
            t_st[...] = t0_ref[...]

    row_c = lax.broadcasted_iota(jnp.int32, (c, 1), 0)
    lane = lax.broadcasted_iota(jnp.int32, (1, LANES), 1)
    y_ref[:, :y_off] = jnp.zeros((c, y_off), y_ref.dtype)

    gates = jnp.zeros((c, 2 * LANES), F32) + bg_ref[...]
    for j in range(n_blk):
        sl = slice(j * QKV_TILE, (j + 1) * QKV_TILE)
        xf = slab_cols(xb_refs, j * QKV_TILE, (j + 1) * QKV_TILE)
        if pad_rows:
            xf = jnp.where(row_c >= pad_rows, xf, 0.0)
        xc = _causal_conv_silu(xf, t_st[:, sl], cw_ref[:, sl], cb_ref[:, sl])
        t_st[:, sl] = xf[c - SUBLANES:]
        xcb = xc.astype(BF16)
        qg = jnp.dot(xcb, wq_ref[j], preferred_element_type=F32)
        vg = jnp.dot(xf.astype(BF16), wv_ref[j], preferred_element_type=F32)
        q_scr[:, sl] = qg[:, :QKV_TILE] * scale
        kt_scr[sl, :] = _dot_nt(wkt_ref[j], xcb)
        v_scr[:, sl] = vg[:, :QKV_TILE].astype(BF16)
        xc_scr[:, sl] = xc
        gates = gates + qg[:, QKV_TILE:] + vg[:, QKV_TILE:]

    ig = gates[:, :LANES]
    lf = _log_sigmoid(gates[:, LANES:])
    if pad_rows:
        ig = jnp.where(row_c >= pad_rows, ig, -jnp.inf)
        lf = jnp.where(row_c >= pad_rows, lf, 0.0)
    causal = _causal_masks(c)
    bcum = _dot_f32(causal.astype(F32), lf)
    a = ig - bcum
    m_old = m_st[...]
    big_m = jnp.maximum(_cummax_rows(a), m_old)
    w_prev = jnp.exp(m_old - big_m)
    e_inv = jnp.exp(-(bcum + big_m))
    m_last = big_m[c - 1:c]
    decay = jnp.exp(m_old - m_last)
    e_col = jnp.exp(a - m_last)
    m_st[...] = bcum[c - 1:c] + m_last
    a2_t = (a * LOG2E).T
    m2 = big_m * LOG2E
    e_t = e_col.T
    yield

    for h in range(n_heads):
        hs = slice(h * dh, (h + 1) * dh)
        wt = jnp.exp2(jnp.where(causal, a2_t[h:h + 1, :] - m2[:, h:h + 1], -jnp.inf))
        kt = kt_scr[hs, :]
        qh = q_scr[:, hs]
        s = _dot(qh, kt) * wt
        lhs = jnp.concatenate([s.astype(BF16), (qh * w_prev[:, h:h + 1]).astype(BF16)], axis=1)
        ones_h = jnp.broadcast_to(jnp.where(lane == h, 1.0, 0.0).astype(BF16), (c, LANES))
        vext = jnp.concatenate([v_scr[:, hs], ones_h], axis=1)
        st = s_st[h]
        nd = jnp.dot(lhs, jnp.concatenate([vext, st.astype(BF16)], axis=0),
                     preferred_element_type=F32)
        r_blk = 1.0 / jnp.maximum(jnp.abs(nd[:, dh:]), e_inv)
        hh = nd[:, :dh] * r_blk[:, h:h + 1]
        wkt = (kt * e_t[h:h + 1, :]).astype(BF16)
        s_st[h] = decay[:, h:h + 1] * st + jnp.dot(wkt, vext, preferred_element_type=F32)
        mu = jnp.mean(hh, axis=1, keepdims=True)
        hc = hh - mu
        var = jnp.mean(hc * hc, axis=1, keepdims=True)
        hn = hc * lax.rsqrt(var + HEAD_NORM_EPS) * nw_ref[:, hs]
        out = (hn + sk_ref[:, hs] * xc_scr[:, hs]) * _silu_of_half(slab_cols(zb_refs, h * dh, (h + 1) * dh))
        y_ref[:, y_off + h * dh:y_off + (h + 1) * dh] = out.astype(y_ref.dtype)


def _state_specs(shapes, emit_state):
    st_in = [pl.BlockSpec((None,) + s, lambda b, i, nd=len(s): (0,) * (nd + 1),
                          pipeline_mode=pl.Buffered(1)) for s in shapes]
    if emit_state:
        st_out = [pl.BlockSpec((None,) + s, lambda b, i, nd=len(s): (b,) + (0,) * nd) for s in shapes]
        return st_in, st_out, []
    return st_in, [], [pltpu.VMEM(s, F32) for s in shapes]


def _mlstm_call(p, col_x, col_z, slab_w, out_width, wts, state0, bsz, n_chunks, pad_rows, emit_state):
    c = CHUNK
    width = wts["cw"].shape[1]
    nh = MLSTM_HEADS
    dh = width // nh
    n_slabs = width // slab_w
    n_sub, steps, row = _step_layout(n_chunks)
    slab_spec = lambda col: pl.BlockSpec((c * n_sub, slab_w), lambda b, i: (row(b, i), col))
    shapes = [(nh, dh, dh + LANES), (1, LANES), (SUBLANES, width)]
    st_in, st_out, st_scr = _state_specs(shapes, emit_state)
    names = ["cw", "cb", "wq", "wkt", "wv", "bg", "nw", "sk"]
    work = [pltpu.VMEM((n_sub, c, width), F32), pltpu.VMEM((n_sub, width, c), F32),
            pltpu.VMEM((n_sub, c, width), BF16), pltpu.VMEM((n_sub, c, width), F32)]
    return pl.pallas_call(
        functools.partial(_mlstm_kernel, n_in_slabs=n_slabs, pad_rows=pad_rows, emit_state=emit_state,
                          n_sub=n_sub),
        grid=(bsz, steps),
        in_specs=[slab_spec(col_x + s) for s in range(n_slabs)]
        + [slab_spec(col_z + s) for s in range(n_slabs)]
        + [_const_spec(wts[n].shape) for n in names] + st_in,
        out_specs=[pl.BlockSpec((c * n_sub, out_width), lambda b, i: (row(b, i), 0))] + st_out,
        out_shape=[jax.ShapeDtypeStruct((bsz * n_chunks * c, out_width), BF16)]
        + ([jax.ShapeDtypeStruct((bsz,) + s, F32) for s in shapes] if emit_state else []),
        scratch_shapes=work + st_scr,
        compiler_params=_params(2),
        name="mlstm_mixer",
    )(*([p] * (2 * n_slabs)), *[wts[n] for n in names], *state0)


def _block_diag_tiles(w, tile):
    nb, b, _ = w.shape
    rows = w.reshape(nb * b // tile, tile, b)
    col = jnp.arange(tile)
    spread = (col[None, :] % b == jnp.arange(b)[:, None]).astype(w.dtype)
    same_block = col[:, None] // b == col[None, :] // b
    return jnp.where(same_block, jnp.einsum("jrb,bc->jrc", rows, spread), 0.0)


def _mlstm_weights(conv_w, conv_b, wq, wk, wv, w_gate, b_gate, norm_w, skip):
    width = conv_w.shape[1]
    nh = MLSTM_HEADS
    n_blk = width // QKV_TILE

    def gate_cols(w_hw, w_g):
        g = jnp.einsum("nio,nog->nig", w_hw, w_g.reshape(-1, QKV_BLOCK, 2 * nh)).reshape(width, 2 * nh)
        fill = jnp.zeros((width, LANES - nh), F32)
        out = jnp.concatenate([g[:, :nh], fill, g[:, nh:], fill], axis=1)
        return out.reshape(n_blk, QKV_TILE, 2 * LANES)

    g_qk = gate_cols(wq, w_gate[:width]) + gate_cols(wk, w_gate[width:2 * width])
    g_v = gate_cols(wv, w_gate[2 * width:])
    wq_t = jnp.concatenate([_block_diag_tiles(wq, QKV_TILE), g_qk], axis=2).astype(BF16)
    wv_t = jnp.concatenate([_block_diag_tiles(wv, QKV_TILE), g_v], axis=2).astype(BF16)
    wk_t = _block_diag_tiles(wk.transpose(0, 2, 1), QKV_TILE).astype(BF16)
    bg = jnp.zeros((1, 2 * LANES), F32)
    bg = bg.at[0, :nh].set(b_gate[:nh]).at[0, LANES:LANES + nh].set(b_gate[nh:])
    return dict(cw=0.5 * conv_w, cb=0.5 * conv_b.reshape(1, width), wq=wq_t, wkt=wk_t, wv=wv_t, bg=bg,
                nw=norm_w.reshape(1, width), sk=skip.reshape(1, width))


def _ssd_kernel(*refs, pad_rows, emit_state, n_sub):
    _pipeline_subchunks(functools.partial(_ssd_chunk, pad_rows=pad_rows, emit_state=emit_state), refs,
                        (0, 1, 2, 3, 4, 19), n_sub)


def _ssd_chunk(*refs, pad_rows, emit_state, slot, is_first):
    (z_ref, xs_ref, bm_ref, cm_ref, dt_ref, cwx_ref, cbx_ref, cwb_ref, cbb_ref, cwc_ref, cbc_ref,
     dtb_ref, a_ref, d_ref, gn_ref, s0_ref, tx0_ref, tb0_ref, tc0_ref, y_ref) = refs[:20]
    if emit_state:
        s_st, tx_st, tb_st, tc_st, xs_scr, bm_scr, cm_scr, y_scr = refs[20:]
    else:
        xs_scr, bm_scr, cm_scr, y_scr, s_st, tx_st, tb_st, tc_st = refs[20:]
    xs_scr, bm_scr, cm_scr, y_scr = (r.at[slot] for r in (xs_scr, bm_scr, cm_scr, y_scr))
    c, inner = xs_ref.shape
    n_pairs = s_st.shape[0]
    pairs_per_group = n_pairs // SSD_GROUPS
    gw = inner // SSD_GROUPS

    if is_first is not None:
        @pl.when(is_first)
        def _():
            s_st[...] = s0_ref[...]
            tx_st[...] = tx0_ref[...]
            tb_st[...] = tb0_ref[...]
            tc_st[...] = tc0_ref[...]

    row_c = lax.broadcasted_iota(jnp.int32, (c, 1), 0)

    def conv_into(dst_scr, src_ref, tail_st, w_ref, b_ref):
        for j in range(src_ref.shape[1] // CONV_BLOCK):
            sl = slice(j * CONV_BLOCK, (j + 1) * CONV_BLOCK)
            x = src_ref[:, sl]
            if pad_rows:
                x = jnp.where(row_c >= pad_rows, x, 0.0)
            dst_scr[:, sl] = _causal_conv_silu(x, tail_st[:, sl], w_ref[:, sl], b_ref[:, sl])
            tail_st[:, sl] = x[c - SUBLANES:]

    conv_into(xs_scr, xs_ref, tx_st, cwx_ref, cbx_ref)
    conv_into(bm_scr, bm_ref, tb_st, cwb_ref, cbb_ref)
    conv_into(cm_scr, cm_ref, tc_st, cwc_ref, cbc_ref)

    dt = jax.nn.softplus(dt_ref[...] + dtb_ref[...])
    if pad_rows:
        dt = jnp.where(row_c >= pad_rows, dt, 0.0)
    causal = _causal_masks(c)
    cum = _dot_f32(causal.astype(F32), dt * a_ref[...])
    cum_t = cum.T
    dt_t = dt.T
    dec = jnp.exp(cum[c - 1:c, :] - cum) * dt
    src2_t = (cum_t - jnp.log(dt_t)) * LOG2E
    cum2 = cum * LOG2E
    e_last = jnp.exp(cum[c - 1:c, :])
    lo = lax.broadcasted_iota(jnp.int32, (1, LANES), 1) < SSD_HEAD_DIM
    yield

    for g in range(SSD_GROUPS):
        gs = slice(g * SSD_STATE, (g + 1) * SSD_STATE)
        cg = cm_scr[:, gs]
        bg_tb = bm_scr[:, gs].T.astype(BF16)
        cb = jnp.dot(cg.astype(BF16), bg_tb, preferred_element_type=F32)
        for pr in range(pairs_per_group):
            pidx = g * pairs_per_group + pr
            ps = slice(pidx * LANES, (pidx + 1) * LANES)
            xp = xs_scr[:, ps]
            xpb = xp.astype(BF16)
            st = s_st[pidx]
            rhs = jnp.concatenate([xpb, st.astype(BF16)], axis=0)
            ys = []
            for h in (2 * pidx, 2 * pidx + 1):
                bc = jnp.broadcast_to(cum2[:, h:h + 1], (c, c))
                seg = jnp.exp2(jnp.where(causal, bc - src2_t[h:h + 1, :], -jnp.inf))
                lhs = jnp.concatenate([(cb * seg).astype(BF16), (cg * jnp.exp2(bc)).astype(BF16)],
                                      axis=1)
                ys.append(jnp.dot(lhs, rhs, preferred_element_type=F32))
            h0 = 2 * pidx
            y_scr[:, ps] = jnp.where(lo, ys[0], ys[1]) + d_ref[:, ps] * xp
            el = jnp.where(lo, e_last[:, h0:h0 + 1], e_last[:, h0 + 1:h0 + 2])
            xdec = xp * jnp.where(lo, dec[:, h0:h0 + 1], dec[:, h0 + 1:h0 + 2])
            s_st[pidx] = el * st + jnp.dot(bg_tb, xdec.astype(BF16), preferred_element_type=F32)

    for g in range(SSD_GROUPS):
        gs = slice(g * gw, (g + 1) * gw)
        yg = y_scr[:, gs] * _silu_of_half(z_ref[:, gs])
        yg = yg * lax.rsqrt(jnp.mean(yg * yg, axis=1, keepdims=True) + NORM_EPS)
        y_ref[:, gs] = (yg * gn_ref[:, gs]).astype(y_ref.dtype)


def _ssd_call(p, pdt, wts, state0, bsz, n_chunks, pad_rows, emit_state):
    c = CHUNK
    inner = wts["d"].shape[1]
    gn = wts["cwb"].shape[1]
    n_pairs = inner // LANES
    n_sub, steps, row = _step_layout(n_chunks)
    cs = c * n_sub
    shapes = [(n_pairs, SSD_STATE, LANES), (SUBLANES, inner), (SUBLANES, gn), (SUBLANES, gn)]
    st_in, st_out, st_scr = _state_specs(shapes, emit_state)
    names = ["cwx", "cbx", "cwb", "cbb", "cwc", "cbc", "dtb", "a", "d", "gn"]
    bcol = 2 * inner // gn
    work = [pltpu.VMEM((n_sub, c, inner), F32), pltpu.VMEM((n_sub, c, gn), F32),
            pltpu.VMEM((n_sub, c, gn), F32), pltpu.VMEM((n_sub, c, inner), F32)]
    return pl.pallas_call(
        functools.partial(_ssd_kernel, pad_rows=pad_rows, emit_state=emit_state, n_sub=n_sub),
        grid=(bsz, steps),
        in_specs=[pl.BlockSpec((cs, inner), lambda b, i: (row(b, i), 0)),
                  pl.BlockSpec((cs, inner), lambda b, i: (row(b, i), 1)),
                  pl.BlockSpec((cs, gn), lambda b, i: (row(b, i), bcol)),
                  pl.BlockSpec((cs, gn), lambda b, i: (row(b, i), bcol + 1)),
                  pl.BlockSpec((cs, LANES), lambda b, i: (row(b, i), 0))]
        + [_const_spec(wts[n].shape) for n in names] + st_in,
        out_specs=[pl.BlockSpec((cs, inner), lambda b, i: (row(b, i), 0))] + st_out,
        out_shape=[jax.ShapeDtypeStruct((bsz * n_chunks * c, inner), BF16)]
        + ([jax.ShapeDtypeStruct((bsz,) + s, F32) for s in shapes] if emit_state else []),
        scratch_shapes=work + st_scr,
        compiler_params=_params(2),
        name="ssd_mixer",
    )(p, p, p, p, pdt, *[wts[n] for n in names], *state0)


def _ssd_weights(conv_w, conv_b, dt_bias, a_log, d, gnorm):
    n_heads = dt_bias.shape[0]
    inner = gnorm.shape[0]
    gn = (conv_w.shape[1] - inner) // 2
    pad = lambda v: jnp.zeros((1, LANES), F32).at[0, :n_heads].set(v)
    cb = 0.5 * conv_b.reshape(1, -1)
    conv_w = 0.5 * conv_w
    return dict(cwx=conv_w[:, :inner], cbx=cb[:, :inner],
                cwb=conv_w[:, inner:inner + gn], cbb=cb[:, inner:inner + gn],
                cwc=conv_w[:, inner + gn:], cbc=cb[:, inner + gn:],
                dtb=pad(dt_bias), a=pad(-jnp.exp(a_log)),
                d=jnp.repeat(d, inner // n_heads).reshape(1, inner), gn=gnorm.reshape(1, inner))


def kernel(x, meta_tokens, ab_norm, ab_w_in, s5_lambda_re, s5_lambda_im, s5_log_dt, s5_b_re, s5_b_im, s5_c_re, s5_c_im, s5_d, s5_glu_w, s5_glu_b, ml_conv_w, ml_conv_b, ml_wq, ml_wk, ml_wv, ml_w_gate, ml_b_gate, ml_norm, ml_skip, ab_w_out, ssd_norm, ssd_w_in, ssd_conv_w, ssd_conv_b, ssd_dt_bias, ssd_a_log, ssd_d, ssd_gnorm, ssd_w_out, final_norm):
    bsz, seq, d_model = x.shape
    assert seq % CHUNK == 0 and N_META <= CHUNK
    n_chunks = seq // CHUNK
    s5_w = s5_d.shape[1]
    ml_w = ml_conv_w.shape[2]
    ssd_inner = ssd_gnorm.shape[1]
    ssd_heads = ssd_dt_bias.shape[1]
    ssd_main = ssd_w_in.shape[2] - ssd_heads
    pad_rows = CHUNK - N_META

    half = lambda n: jnp.full((n,), 0.5, F32)
    one = lambda n: jnp.ones((n,), F32)
    w_in0 = (ab_w_in[0] * jnp.concatenate([one(s5_w), half(s5_w), one(ml_w), half(ml_w)])).astype(BF16)
    slab_w = s5_w
    w_out0 = ab_w_out[0].astype(BF16)
    s5_wts = _s5_weights(s5_lambda_re[0], s5_lambda_im[0], s5_log_dt[0], s5_b_re[0], s5_b_im[0],
                         s5_c_re[0], s5_c_im[0], s5_d[0], s5_glu_w[0], s5_glu_b[0])
    ml_wts = _mlstm_weights(ml_conv_w[0], ml_conv_b[0], ml_wq[0], ml_wk[0], ml_wv[0],
                            ml_w_gate[0], ml_b_gate[0], ml_norm[0], ml_skip[0])
    w_in1_t = jnp.swapaxes(ssd_w_in[0], 0, 1)
    w_in1_main = (w_in1_t[:ssd_main]
                  * jnp.concatenate([half(ssd_inner), one(ssd_main - ssd_inner)])[:, None]).astype(BF16)
    w_in1_dt = jnp.zeros((LANES, d_model), F32).at[:ssd_heads].set(w_in1_t[ssd_main:]).astype(BF16)
    w_out1 = ssd_w_out[0].astype(BF16)
    ssd_wts = _ssd_weights(ssd_conv_w[0], ssd_conv_b[0], ssd_dt_bias[0], ssd_a_log[0], ssd_d[0],
                           ssd_gnorm[0])

    n_tiles = s5_wts["apr"].shape[0]
    dh = ml_w // MLSTM_HEADS
    zeros = lambda *s: jnp.zeros(s, F32)
    s5_zero = (zeros(1, n_tiles, 1, LANES), zeros(1, n_tiles, 1, LANES))
    ml_zero = (zeros(1, MLSTM_HEADS, dh, dh + LANES), zeros(1, 1, LANES), zeros(1, SUBLANES, ml_w))
    gn = (ssd_main - 2 * ssd_inner) // 2
    ssd_zero = (zeros(1, ssd_inner // LANES, SSD_STATE, LANES), zeros(1, SUBLANES, ssd_inner),
                zeros(1, SUBLANES, gn), zeros(1, SUBLANES, gn))

    def layer0(h, b, nc, pad, s5_state, ml_state, emit_state):
        p = _norm_proj(h, ab_norm[0], w_in0, name="ab_in_proj")[0]
        y, *ml_out = _mlstm_call(p, 2 * s5_w // slab_w, (2 * s5_w + ml_w) // slab_w, slab_w,
                                 s5_w + ml_w, ml_wts, ml_state, b, nc, pad, emit_state)
        y, sr, si = _s5_call(p, 0, 1, y, 0, s5_wts, s5_state, b, nc, pad)
        h = _res_proj(y, w_out0, h, name="ab_out_proj")
        return h, (sr, si), tuple(ml_out)

    def layer1_mixer(h, b, nc, pad, ssd_state, emit_state):
        p, pdt = _norm_proj(h, ssd_norm[0], w_in1_main, w_in1_dt, w_transposed=True,
                            name="ssd_in_proj")
        y, *st = _ssd_call(p, pdt, ssd_wts, ssd_state, b, nc, pad, emit_state)
        return y, tuple(st)

    hm = jnp.concatenate([zeros(pad_rows, d_model), meta_tokens.astype(F32)], axis=0)
    hm, s5_state, ml_state = layer0(hm, 1, 1, pad_rows, s5_zero, ml_zero, True)
    _, ssd_state = layer1_mixer(hm, 1, 1, pad_rows, ssd_zero, True)

    h = x.reshape(bsz * seq, d_model)
    h, _, _ = layer0(h, bsz, n_chunks, 0, s5_state, ml_state, False)
    y, _ = layer1_mixer(h, bsz, n_chunks, 0, ssd_state, False)
    out = _res_proj_norm(y, w_out1, h, final_norm, name="ssd_out_proj_norm")
    return out.reshape(bsz, seq, d_model)
```

```python
import functools

import jax
import jax.numpy as jnp
from jax import lax
from jax.experimental import pallas as pl
from jax.experimental.pallas import tpu as pltpu

F32 = jnp.float32
BF16 = jnp.bfloat16

N_META = 16
CHUNK = 128
NORM_EPS = 1e-6
HEAD_NORM_EPS = 1e-5
LANES = 128
SUBLANES = 8
VMEM_LIMIT = 56 * 1024 * 1024

S5_GROUP_SIZE = 16
S5_STATE = 64
S5_OCT = 8
MLSTM_HEADS = 8
QKV_BLOCK = 4
QKV_TILE = 256
SSD_HEAD_DIM = 64
SSD_STATE = 128
SSD_GROUPS = 8
CONV_BLOCK = 512
LOG2E = 1.4426950408889634
NORM_PROJ_TN = 1024
SUBCHUNKS = 2


def _params(n_grid):
    return pltpu.CompilerParams(dimension_semantics=("arbitrary",) * n_grid,
                                vmem_limit_bytes=VMEM_LIMIT)


def _const_spec(shape):
    nd = len(shape)
    return pl.BlockSpec(shape, lambda *_: (0,) * nd, pipeline_mode=pl.Buffered(1))


def _dot(a, b):
    return jnp.dot(a.astype(BF16), b.astype(BF16), preferred_element_type=F32)


def _dot_nt(a, b):
    return lax.dot_general(a.astype(BF16), b.astype(BF16), (((1,), (1,)), ((), ())),
                           preferred_element_type=F32)


def _dot_f32(a, b):
    return jnp.dot(a, b, preferred_element_type=F32, precision=lax.Precision.HIGHEST)


def _sigmoid(x):
    return 0.5 * (1.0 + jnp.tanh(0.5 * x))


def _silu_of_half(h):
    return h + h * jnp.tanh(h)


def _shift_rows(x, d, prev8):
    rolled = pltpu.roll(x, d, axis=0)
    row8 = lax.broadcasted_iota(jnp.int32, (SUBLANES, x.shape[1]), 0)
    head = jnp.where(row8 < d, pltpu.roll(prev8, d, axis=0), rolled[:SUBLANES])
    if x.shape[0] == SUBLANES:
        return head
    return jnp.concatenate([head, rolled[SUBLANES:]], axis=0)


def _causal_conv_silu(x, tail8, w_half, b_half):
    assert w_half.shape[0] == 4
    w0, w1, w2, w3 = (w_half[i:i + 1, :] for i in range(4))
    x2 = _shift_rows(x, 2, tail8)
    u = x * w2 + x2 * w0
    u_tail = tail8 * w2 + pltpu.roll(tail8, 2, axis=0) * w0
    h = x * w3 + x2 * w1 + b_half + _shift_rows(u, 1, u_tail)
    return h + h * jnp.tanh(h)


def _causal_masks(c):
    r_i = lax.broadcasted_iota(jnp.int32, (c, c), 0)
    c_i = lax.broadcasted_iota(jnp.int32, (c, c), 1)
    return r_i >= c_i


def _pipeline_subchunks(chunk_gen, refs, row_idx, n_sub):
    first_step = pl.program_id(1) == 0
    chunks = []
    for s in range(n_sub):
        rows = pl.ds(s * CHUNK, CHUNK)
        sub = [r.at[rows, :] if k in row_idx else r for k, r in enumerate(refs)]
        chunks.append(chunk_gen(*sub, slot=s, is_first=first_step if s == 0 else None))
    for chunk in chunks:
        next(chunk)
    for chunk in chunks:
        for _ in chunk:
            pass


def _step_layout(n_chunks):
    n_sub = SUBCHUNKS if n_chunks % SUBCHUNKS == 0 else 1
    steps = n_chunks // n_sub
    return n_sub, steps, lambda b, i: b * steps + i


def _norm_proj_kernel(a_ref, g_ref, w_ref, *rest, has_extra, w_transposed):
    if has_extra:
        wx_ref, o_ref, ox_ref, a_scr = rest
    else:
        o_ref, a_scr = rest
    tm = a_ref.shape[0]
    slab = min(tm, CHUNK)

    def mm(lhs, rhs):
        if w_transposed:
            return lax.dot_general(lhs, rhs, (((1,), (1,)), ((), ())), preferred_element_type=F32)
        return jnp.dot(lhs, rhs, preferred_element_type=F32)

    n_part = 2 if tm >= 4 * slab else 1
    part = tm // n_part

    @pl.when(pl.program_id(1) == 0)
    def _():
        for h in range(n_part):
            for s in range(part // slab):
                rows = slice(h * part + s * slab, h * part + (s + 1) * slab)
                x = a_ref[rows, :]
                y = x * lax.rsqrt(jnp.mean(x * x, axis=-1, keepdims=True) + NORM_EPS)
                a_scr[rows, :] = (y * g_ref[...]).astype(BF16)
            prow = slice(h * part, (h + 1) * part)
            o_ref[prow, :] = mm(a_scr[prow, :], w_ref[...])
        if has_extra:
            ox_ref[...] = mm(a_scr[...], wx_ref[...])

    @pl.when(pl.program_id(1) != 0)
    def _():
        o_ref[...] = mm(a_scr[...], w_ref[...])


def _norm_proj(a, g, w, w_extra=None, w_transposed=False, name="norm_proj"):
    m, k = a.shape
    n = w.shape[0] if w_transposed else w.shape[1]
    tm = min(m, 1024)
    tn = NORM_PROJ_TN
    w_spec = (pl.BlockSpec((tn, k), lambda i, j: (j, 0)) if w_transposed
              else pl.BlockSpec((k, tn), lambda i, j: (0, j)))
    in_specs = [pl.BlockSpec((tm, k), lambda i, j: (i, 0)), _const_spec((1, k)), w_spec]
    out_specs = [pl.BlockSpec((tm, tn), lambda i, j: (i, j))]
    out_shape = [jax.ShapeDtypeStruct((m, n), F32)]
    args = [a, g.reshape(1, k), w]
    if w_extra is not None:
        nx = w_extra.shape[0] if w_transposed else w_extra.shape[1]
        in_specs.append(_const_spec(w_extra.shape))
        out_specs.append(pl.BlockSpec((tm, nx), lambda i, j: (i, 0)))
        out_shape.append(jax.ShapeDtypeStruct((m, nx), F32))
        args.append(w_extra)
    return pl.pallas_call(
        functools.partial(_norm_proj_kernel, has_extra=w_extra is not None,
                          w_transposed=w_transposed),
        grid=(m // tm, n // tn),
        in_specs=in_specs, out_specs=out_specs, out_shape=out_shape,
        scratch_shapes=[pltpu.VMEM((tm, k), BF16)],
        compiler_params=_params(2),
        name=name,
    )(*args)


def _res_proj_kernel(a_ref, w_ref, r_ref, o_ref):
    o_ref[...] = jnp.dot(a_ref[...], w_ref[...], preferred_element_type=F32) + r_ref[...]


def _res_proj(a, w, res, name="res_proj"):
    m, k = a.shape
    n = w.shape[1]
    tm = min(m, 1024)
    tn = 1024
    return pl.pallas_call(
        _res_proj_kernel,
        grid=(m // tm, n // tn),
        in_specs=[pl.BlockSpec((tm, k), lambda i, j: (i, 0)),
                  pl.BlockSpec((k, tn), lambda i, j: (0, j)),
                  pl.BlockSpec((tm, tn), lambda i, j: (i, j))],
        out_specs=pl.BlockSpec((tm, tn), lambda i, j: (i, j)),
        out_shape=jax.ShapeDtypeStruct((m, n), F32),
        compiler_params=_params(2),
        name=name,
    )(a, w, res)


def _res_proj_norm_kernel(a_ref, w_ref, r_ref, g_ref, o_ref):
    tm = a_ref.shape[0]
    slab = min(tm, CHUNK)
    for s in range(tm // slab):
        rows = slice(s * slab, (s + 1) * slab)
        x = jnp.dot(a_ref[rows, :], w_ref[...], preferred_element_type=F32) + r_ref[rows, :]
        y = x * lax.rsqrt(jnp.mean(x * x, axis=-1, keepdims=True) + NORM_EPS)
        o_ref[rows, :] = y * g_ref[...]


def _res_proj_norm(a, w, res, g, name="res_proj_norm"):
    m, k = a.shape
    n = w.shape[1]
    tm = min(m, 512)
    return pl.pallas_call(
        _res_proj_norm_kernel,
        grid=(m // tm,),
        in_specs=[pl.BlockSpec((tm, k), lambda i: (i, 0)), _const_spec((k, n)),
                  pl.BlockSpec((tm, n), lambda i: (i, 0)), _const_spec((1, n))],
        out_specs=pl.BlockSpec((tm, n), lambda i: (i, 0)),
        out_shape=jax.ShapeDtypeStruct((m, n), F32),
        compiler_params=_params(1),
        name=name,
    )(a, w, res, g.reshape(1, n))


def _s5_kernel(*refs, pad_rows, n_sub):
    _pipeline_subchunks(functools.partial(_s5_chunk, pad_rows=pad_rows), refs, (0, 1, 12), n_sub)


def _s5_chunk(u_ref, z_ref, wb_ref, wc_ref, apr_ref, api_ref, d_ref, gw_ref, gb_ref,
              sr0_ref, si0_ref, yin_ref, y_ref, sr_ref, si_ref, out_scr, *, pad_rows, slot, is_first):
    del yin_ref
    c = u_ref.shape[0]
    seg = c // SUBLANES
    n_tiles = apr_ref.shape[0]
    tiles_per_oct = n_tiles // S5_OCT
    half = tiles_per_oct * LANES

    if is_first is not None:
        @pl.when(is_first)
        def _():
            sr_ref[...] = sr0_ref[...]
            si_ref[...] = si0_ref[...]

    out_scr = out_scr.at[slot]
    n_slabs = out_scr.shape[0]

    def load_permuted(ref):
        for k in range(n_slabs):
            out_scr[k] = ref[:, k * LANES:(k + 1) * LANES]
        return jnp.concatenate(
            [jnp.concatenate([out_scr[k, pl.ds(i, SUBLANES, stride=seg), :] for i in range(seg)], axis=0)
             for k in range(n_slabs)], axis=1)

    u = load_permuted(u_ref)
    if pad_rows:
        row = lax.broadcasted_iota(jnp.int32, (c, 1), 0)
        t_of_row = (row & (SUBLANES - 1)) * seg + (row >> 3)
        u = jnp.where(t_of_row >= pad_rows, u, 0.0)

    zeros8 = jnp.zeros((SUBLANES, LANES), F32)

    def cmul_add(xr, xi, pr, pi, yr, yi):
        return xr + pr * yr - pi * yi, xi + pr * yi + pi * yr

    def scan_tile(j, bur, bui):
        blk = lambda i: slice(i * SUBLANES, (i + 1) * SUBLANES)
        a1r, a1i = apr_ref[j, 0:1, :], api_ref[j, 0:1, :]
        br, bi = bur[blk(0)], bui[blk(0)]
        loc = [(br, bi)]
        for i in range(1, seg):
            br, bi = cmul_add(bur[blk(i)], bui[blk(i)], a1r, a1i, br, bi)
            loc.append((br, bi))
        fr, fi = br, bi
        for d in (1, 2, 4):
            p = seg * d - 1
            fr, fi = cmul_add(fr, fi, apr_ref[j, p:p + 1, :], api_ref[j, p:p + 1, :],
                              _shift_rows(fr, d, zeros8), _shift_rows(fi, d, zeros8))
        cr, ci = sr_ref[j], si_ref[j]
        fr, fi = cmul_add(fr, fi, apr_ref[j, pl.ds(seg - 1, SUBLANES, stride=seg), :],
                          api_ref[j, pl.ds(seg - 1, SUBLANES, stride=seg), :], cr, ci)
        gr = _shift_rows(fr, 1, jnp.broadcast_to(cr, (SUBLANES, LANES)))
        gi = _shift_rows(fi, 1, jnp.broadcast_to(ci, (SUBLANES, LANES)))
        xs = [cmul_add(loc[i][0], loc[i][1], apr_ref[j, i:i + 1, :], api_ref[j, i:i + 1, :], gr, gi)
              for i in range(seg)]
        sr_ref[j] = fr[SUBLANES - 1:SUBLANES]
        si_ref[j] = fi[SUBLANES - 1:SUBLANES]
        return (jnp.concatenate([x[0] for x in xs], axis=0),
                jnp.concatenate([x[1] for x in xs], axis=0))

    ys = []
    for k in range(S5_OCT):
        bu = _dot(u[:, k * LANES:(k + 1) * LANES], wb_ref[k])
        xrs, xis = [], []
        for q in range(tiles_per_oct):
            xr, xi = scan_tile(k * tiles_per_oct + q, bu[:, q * LANES:(q + 1) * LANES],
                               bu[:, half + q * LANES:half + (q + 1) * LANES])
            xrs.append(xr)
            xis.append(xi)
        ys.append(_dot(jnp.concatenate(xrs + xis, axis=1), wc_ref[k]))
    yield
    y = jnp.concatenate(ys, axis=1) + d_ref[...] * u
    g = jax.nn.gelu(y)
    gate = _sigmoid(_dot(g, gw_ref[...]) + gb_ref[...])
    res = g * gate
    for k in range(n_slabs):
        cols = slice(k * LANES, (k + 1) * LANES)
        for i in range(seg):
            out_scr[k, pl.ds(i, SUBLANES, stride=seg), :] = res[i * SUBLANES:(i + 1) * SUBLANES, cols]
        y_ref[:, cols] = (out_scr[k] * _silu_of_half(z_ref[:, cols])).astype(y_ref.dtype)


def _s5_call(p, col_u, col_z, y_buf, col_y, wts, state0, bsz, n_chunks, pad_rows):
    c = CHUNK
    width = wts["d"].shape[1]
    n_tiles = wts["apr"].shape[0]
    n_sub, steps, row = _step_layout(n_chunks)
    st_spec_in = pl.BlockSpec((None, n_tiles, 1, LANES), lambda b, i: (0, 0, 0, 0),
                              pipeline_mode=pl.Buffered(1))
    st_spec_out = pl.BlockSpec((None, n_tiles, 1, LANES), lambda b, i: (b, 0, 0, 0))
    st_shape = jax.ShapeDtypeStruct((bsz, n_tiles, 1, LANES), F32)
    names = ["wb", "wc", "apr", "api", "d", "gw", "gb"]
    return pl.pallas_call(
        functools.partial(_s5_kernel, pad_rows=pad_rows, n_sub=n_sub),
        grid=(bsz, steps),
        in_specs=[pl.BlockSpec((c * n_sub, width), lambda b, i: (row(b, i), col_u)),
                  pl.BlockSpec((c * n_sub, width), lambda b, i: (row(b, i), col_z))]
        + [_const_spec(wts[n].shape) for n in names]
        + [st_spec_in, st_spec_in, pl.BlockSpec(memory_space=pl.ANY)],
        out_specs=[pl.BlockSpec((c * n_sub, width), lambda b, i: (row(b, i), col_y)),
                   st_spec_out, st_spec_out],
        out_shape=[jax.ShapeDtypeStruct(y_buf.shape, y_buf.dtype), st_shape, st_shape],
        scratch_shapes=[pltpu.VMEM((n_sub, width // LANES, c, LANES), F32)],
        input_output_aliases={len(names) + 4: 0},
        compiler_params=_params(2),
        name="s5_mixer",
    )(p, p, *[wts[n] for n in names], state0[0], state0[1], y_buf)


def _s5_weights(lam_re, lam_im, log_dt, b_re, b_im, c_re, c_im, d, glu_w, glu_b):
    g_n, p_n = lam_re.shape
    dt = jnp.exp(log_dt)[:, None]
    mag = jnp.exp(lam_re * dt)
    ar, ai = mag * jnp.cos(lam_im * dt), mag * jnp.sin(lam_im * dt)
    den = lam_re * lam_re + lam_im * lam_im
    qr = ((ar - 1.0) * lam_re + ai * lam_im) / den
    qi = (ai * lam_re - (ar - 1.0) * lam_im) / den
    bbr = qr[..., None] * b_re - qi[..., None] * b_im
    bbi = qr[..., None] * b_im + qi[..., None] * b_re
    n_oct = g_n // S5_OCT
    eye = jnp.eye(S5_OCT, dtype=F32)

    def in_proj(w):
        w = w.reshape(n_oct, S5_OCT, p_n, S5_GROUP_SIZE)
        return jnp.einsum("kgph,gG->kghGp", w, eye).reshape(n_oct, S5_OCT * S5_GROUP_SIZE, S5_OCT * p_n)

    def out_proj(w):
        w = w.reshape(n_oct, S5_OCT, S5_GROUP_SIZE, p_n)
        return jnp.einsum("kghp,gG->kgpGh", w, eye).reshape(n_oct, S5_OCT * p_n, S5_OCT * S5_GROUP_SIZE)

    wb = jnp.concatenate([in_proj(bbr), in_proj(bbi)], axis=2).astype(BF16)
    wc = jnp.concatenate([out_proj(c_re), out_proj(-c_im)], axis=1).astype(BF16)
    pr, pi = ar.reshape(1, -1), ai.reshape(1, -1)
    while pr.shape[0] < CHUNK:
        lr_, li_ = pr[-1:], pi[-1:]
        pr, pi = (jnp.concatenate([pr, pr * lr_ - pi * li_], axis=0),
                  jnp.concatenate([pi, pr * li_ + pi * lr_], axis=0))
    n_tiles = g_n * p_n // LANES
    tile = lambda a: a.reshape(CHUNK, n_tiles, LANES).transpose(1, 0, 2)
    width = d.shape[0]
    return dict(wb=wb, wc=wc, apr=tile(pr), api=tile(pi), d=d.reshape(1, width),
                gw=glu_w.astype(BF16), gb=glu_b.reshape(1, width))


def _log_sigmoid(x):
    return jnp.minimum(x, 0.0) - jnp.log1p(jnp.exp(-jnp.abs(x)))


def _cummax_rows(x):
    c = x.shape[0]
    neg8 = jnp.full((SUBLANES, x.shape[1]), -jnp.inf, F32)
    d = 1
    while d < c:
        if d < SUBLANES:
            sh = _shift_rows(x, d, neg8)
        else:
            sh = jnp.concatenate([jnp.full((d, x.shape[1]), -jnp.inf, F32), x[:c - d]], axis=0)
        x = jnp.maximum(x, sh)
        d *= 2
    return x


def _mlstm_kernel(*refs, n_in_slabs, pad_rows, emit_state, n_sub):
    row_idx = tuple(range(2 * n_in_slabs)) + (2 * n_in_slabs + 11,)
    _pipeline_subchunks(functools.partial(_mlstm_chunk, n_in_slabs=n_in_slabs, pad_rows=pad_rows,
                                          emit_state=emit_state), refs, row_idx, n_sub)


def _mlstm_chunk(*refs, n_in_slabs, pad_rows, emit_state, slot, is_first):
    xb_refs = refs[:n_in_slabs]
    zb_refs = refs[n_in_slabs:2 * n_in_slabs]
    k = 2 * n_in_slabs
    (cw_ref, cb_ref, wq_ref, wkt_ref, wv_ref, bg_ref, nw_ref, sk_ref,
     s0_ref, m0_ref, t0_ref, y_ref) = refs[k:k + 12]
    if emit_state:
        s_st, m_st, t_st, q_scr, kt_scr, v_scr, xc_scr = refs[k + 12:]
    else:
        q_scr, kt_scr, v_scr, xc_scr, s_st, m_st, t_st = refs[k + 12:]
    q_scr, kt_scr, v_scr, xc_scr = (r.at[slot] for r in (q_scr, kt_scr, v_scr, xc_scr))
    c, slab_w = xb_refs[0].shape
    n_heads, dh = s_st.shape[0], s_st.shape[1]
    n_blk = wq_ref.shape[0]
    y_off = y_ref.shape[1] - n_heads * dh
    scale = dh ** -0.5

    def slab_cols(slab_refs, lo, hi):
        parts = []
        while lo < hi:
            end = min(hi, (lo // slab_w + 1) * slab_w)
            parts.append(slab_refs[lo // slab_w][:, lo % slab_w:lo % slab_w + (end - lo)])
            lo = end
        return parts[0] if len(parts) == 1 else jnp.concatenate(parts, axis=1)

    if is_first is not None:
        @pl.when(is_first)
        def _():
            s_st[...] = s0_ref[...]
            m_st[...] = m0_ref[...]
            t_st[...] = t0_ref[...]

    row_c = lax.broadcasted_iota(jnp.int32, (c, 1), 0)
    lane = lax.broadcasted_iota(jnp.int32, (1, LANES), 1)
    y_ref[:, :y_off] = jnp.zeros((c, y_off), y_ref.dtype)

    gates = jnp.zeros((c, 2 * LANES), F32) + bg_ref[...]
    for j in range(n_blk):
        sl = slice(j * QKV_TILE, (j + 1) * QKV_TILE)
        xf = slab_cols(xb_refs, j * QKV_TILE, (j + 1) * QKV_TILE)
        if pad_rows:
            xf = jnp.where(row_c >= pad_rows, xf, 0.0)
        xc = _causal_conv_silu(xf, t_st[:, sl], cw_ref[:, sl], cb_ref[:, sl])
        t_st[:, sl] = xf[c - SUBLANES:]
        xcb = xc.astype(BF16)
        qg = jnp.dot(xcb, wq_ref[j], preferred_element_type=F32)
        vg = jnp.dot(xf.astype(BF16), wv_ref[j], preferred_element_type=F32)
        q_scr[:, sl] = qg[:, :QKV_TILE] * scale
        kt_scr[sl, :] = _dot_nt(wkt_ref[j], xcb)
        v_scr[:, sl] = vg[:, :QKV_TILE].astype(BF16)
        xc_scr[:, sl] = xc
        gates = gates + qg[:, QKV_TILE:] + vg[:, QKV_TILE:]

    ig = gates[:, :LANES]
    lf = _log_sigmoid(gates[:, LANES:])
    if pad_rows:
        ig = jnp.where(row_c >= pad_rows, ig, -jnp.inf)
        lf = jnp.where(row_c >= pad_rows, lf, 0.0)
    causal = _causal_masks(c)
    bcum = _dot_f32(causal.astype(F32), lf)
    a = ig - bcum
    m_old = m_st[...]
    big_m = jnp.maximum(_cummax_rows(a), m_old)
    w_prev = jnp.exp(m_old - big_m)
    e_inv = jnp.exp(-(bcum + big_m))
    m_last = big_m[c - 1:c]
    decay = jnp.exp(m_old - m_last)
    e_col = jnp.exp(a - m_last)
    m_st[...] = bcum[c - 1:c] + m_last
    a2_t = (a * LOG2E).T
    m2 = big_m * LOG2E
    e_t = e_col.T
    yield

    for h in range(n_heads):
        hs = slice(h * dh, (h + 1) * dh)
        wt = jnp.exp2(jnp.where(causal, a2_t[h:h + 1, :] - m2[:, h:h + 1], -jnp.inf))
        kt = kt_scr[hs, :]
        qh = q_scr[:, hs]
        s = _dot(qh, kt) * wt
        lhs = jnp.concatenate([s.astype(BF16), (qh * w_prev[:, h:h + 1]).astype(BF16)], axis=1)
        ones_h = jnp.broadcast_to(jnp.where(lane == h, 1.0, 0.0).astype(BF16), (c, LANES))
        vext = jnp.concatenate([v_scr[:, hs], ones_h], axis=1)
        st = s_st[h]
        nd = jnp.dot(lhs, jnp.concatenate([vext, st.astype(BF16)], axis=0),
                     preferred_element_type=F32)
        r_blk = 1.0 / jnp.maximum(jnp.abs(nd[:, dh:]), e_inv)
        hh = nd[:, :dh] * r_blk[:, h:h + 1]
        wkt = (kt * e_t[h:h + 1, :]).astype(BF16)
        s_st[h] = decay[:, h:h + 1] * st + jnp.dot(wkt, vext, preferred_element_type=F32)
        mu = jnp.mean(hh, axis=1, keepdims=True)
        hc = hh - mu
        var = jnp.mean(hc * hc, axis=1, keepdims=True)
        hn = hc * lax.rsqrt(var + HEAD_NORM_EPS) * nw_ref[:, hs]
        out = (hn + sk_ref[:, hs] * xc_scr[:, hs]) * _silu_of_half(slab_cols(zb_refs, h * dh, (h + 1) * dh))
        y_ref[:, y_off + h * dh:y_off + (h + 1) * dh] = out.astype(y_ref.dtype)


def _state_specs(shapes, emit_state):
    st_in = [pl.BlockSpec((None,) + s, lambda b, i, nd=len(s): (0,) * (nd + 1),
                          pipeline_mode=pl.Buffered(1)) for s in shapes]
    if emit_state:
        st_out = [pl.BlockSpec((None,) + s, lambda b, i, nd=len(s): (b,) + (0,) * nd) for s in shapes]
        return st_in, st_out, []
    return st_in, [], [pltpu.VMEM(s, F32) for s in shapes]


def _mlstm_call(p, col_x, col_z, slab_w, out_width, wts, state0, bsz, n_chunks, pad_rows, emit_state):
    c = CHUNK
    width = wts["cw"].shape[1]
    nh = MLSTM_HEADS
    dh = width // nh
    n_slabs = width // slab_w
    n_sub, steps, row = _step_layout(n_chunks)
    slab_spec = lambda col: pl.BlockSpec((c * n_sub, slab_w), lambda b, i: (row(b, i), col))
    shapes = [(nh, dh, dh + LANES), (1, LANES), (SUBLANES, width)]
    st_in, st_out, st_scr = _state_specs(shapes, emit_state)
    names = ["cw", "cb", "wq", "wkt", "wv", "bg", "nw", "sk"]
    work = [pltpu.VMEM((n_sub, c, width), F32), pltpu.VMEM((n_sub, width, c), F32),
            pltpu.VMEM((n_sub, c, width), BF16), pltpu.VMEM((n_sub, c, width), F32)]
    return pl.pallas_call(
        functools.partial(_mlstm_kernel, n_in_slabs=n_slabs, pad_rows=pad_rows, emit_state=emit_state,
                          n_sub=n_sub),
        grid=(bsz, steps),
        in_specs=[slab_spec(col_x + s) for s in range(n_slabs)]
        + [slab_spec(col_z + s) for s in range(n_slabs)]
        + [_const_spec(wts[n].shape) for n in names] + st_in,
        out_specs=[pl.BlockSpec((c * n_sub, out_width), lambda b, i: (row(b, i), 0))] + st_out,
        out_shape=[jax.ShapeDtypeStruct((bsz * n_chunks * c, out_width), BF16)]
        + ([jax.ShapeDtypeStruct((bsz,) + s, F32) for s in shapes] if emit_state else []),
        scratch_shapes=work + st_scr,
        compiler_params=_params(2),
        name="mlstm_mixer",
    )(*([p] * (2 * n_slabs)), *[wts[n] for n in names], *state0)


def _block_diag_tiles(w, tile):
    nb, b, _ = w.shape
    rows = w.reshape(nb * b // tile, tile, b)
    col = jnp.arange(tile)
    spread = (col[None, :] % b == jnp.arange(b)[:, None]).astype(w.dtype)
    same_block = col[:, None] // b == col[None, :] // b
    return jnp.where(same_block, jnp.einsum("jrb,bc->jrc", rows, spread), 0.0)


def _mlstm_weights(conv_w, conv_b, wq, wk, wv, w_gate, b_gate, norm_w, skip):
    width = conv_w.shape[1]
    nh = MLSTM_HEADS
    n_blk = width // QKV_TILE

    def gate_cols(w_hw, w_g):
        g = jnp.einsum("nio,nog->nig", w_hw, w_g.reshape(-1, QKV_BLOCK, 2 * nh)).reshape(width, 2 * nh)
        fill = jnp.zeros((width, LANES - nh), F32)
        out = jnp.concatenate([g[:, :nh], fill, g[:, nh:], fill], axis=1)
        return out.reshape(n_blk, QKV_TILE, 2 * LANES)

    g_qk = gate_cols(wq, w_gate[:width]) + gate_cols(wk, w_gate[width:2 * width])
    g_v = gate_cols(wv, w_gate[2 * width:])
    wq_t = jnp.concatenate([_block_diag_tiles(wq, QKV_TILE), g_qk], axis=2).astype(BF16)
    wv_t = jnp.concatenate([_block_diag_tiles(wv, QKV_TILE), g_v], axis=2).astype(BF16)
    wk_t = _block_diag_tiles(wk.transpose(0, 2, 1), QKV_TILE).astype(BF16)
    bg = jnp.zeros((1, 2 * LANES), F32)
    bg = bg.at[0, :nh].set(b_gate[:nh]).at[0, LANES:LANES + nh].set(b_gate[nh:])
    return dict(cw=0.5 * conv_w, cb=0.5 * conv_b.reshape(1, width), wq=wq_t, wkt=wk_t, wv=wv_t, bg=bg,
                nw=norm_w.reshape(1, width), sk=skip.reshape(1, width))


def _ssd_kernel(*refs, pad_rows, emit_state, n_sub):
    _pipeline_subchunks(functools.partial(_ssd_chunk, pad_rows=pad_rows, emit_state=emit_state), refs,
                        (0, 1, 2, 3, 4, 19), n_sub)


def _ssd_chunk(*refs, pad_rows, emit_state, slot, is_first):
    (z_ref, xs_ref, bm_ref, cm_ref, dt_ref, cwx_ref, cbx_ref, cwb_ref, cbb_ref, cwc_ref, cbc_ref,
     dtb_ref, a_ref, d_ref, gn_ref, s0_ref, tx0_ref, tb0_ref, tc0_ref, y_ref) = refs[:20]
    if emit_state:
        s_st, tx_st, tb_st, tc_st, xs_scr, bm_scr, cm_scr, y_scr = refs[20:]
    else:
        xs_scr, bm_scr, cm_scr, y_scr, s_st, tx_st, tb_st, tc_st = refs[20:]
    xs_scr, bm_scr, cm_scr, y_scr = (r.at[slot] for r in (xs_scr, bm_scr, cm_scr, y_scr))
    c, inner = xs_ref.shape
    n_pairs = s_st.shape[0]
    pairs_per_group = n_pairs // SSD_GROUPS
    gw = inner // SSD_GROUPS

    if is_first is not None:
        @pl.when(is_first)
        def _():
            s_st[...] = s0_ref[...]
            tx_st[...] = tx0_ref[...]
            tb_st[...] = tb0_ref[...]
            tc_st[...] = tc0_ref[...]

    row_c = lax.broadcasted_iota(jnp.int32, (c, 1), 0)

    def conv_into(dst_scr, src_ref, tail_st, w_ref, b_ref):
        for j in range(src_ref.shape[1] // CONV_BLOCK):
            sl = slice(j * CONV_BLOCK, (j + 1) * CONV_BLOCK)
            x = src_ref[:, sl]
            if pad_rows:
                x = jnp.where(row_c >= pad_rows, x, 0.0)
            dst_scr[:, sl] = _causal_conv_silu(x, tail_st[:, sl], w_ref[:, sl], b_ref[:, sl])
            tail_st[:, sl] = x[c - SUBLANES:]

    conv_into(xs_scr, xs_ref, tx_st, cwx_ref, cbx_ref)
    conv_into(bm_scr, bm_ref, tb_st, cwb_ref, cbb_ref)
    conv_into(cm_scr, cm_ref, tc_st, cwc_ref, cbc_ref)

    dt = jax.nn.softplus(dt_ref[...] + dtb_ref[...])
    if pad_rows:
        dt = jnp.where(row_c >= pad_rows, dt, 0.0)
    causal = _causal_masks(c)
    cum = _dot_f32(causal.astype(F32), dt * a_ref[...])
    cum_t = cum.T
    dt_t = dt.T
    dec = jnp.exp(cum[c - 1:c, :] - cum) * dt
    src2_t = (cum_t - jnp.log(dt_t)) * LOG2E
    cum2 = cum * LOG2E
    e_last = jnp.exp(cum[c - 1:c, :])
    lo = lax.broadcasted_iota(jnp.int32, (1, LANES), 1) < SSD_HEAD_DIM
    yield

    for g in range(SSD_GROUPS):
        gs = slice(g * SSD_STATE, (g + 1) * SSD_STATE)
        cg = cm_scr[:, gs]
        bg_tb = bm_scr[:, gs].T.astype(BF16)
        cb = jnp.dot(cg.astype(BF16), bg_tb, preferred_element_type=F32)
        for pr in range(pairs_per_group):
            pidx = g * pairs_per_group + pr
            ps = slice(pidx * LANES, (pidx + 1) * LANES)
            xp = xs_scr[:, ps]
            xpb = xp.astype(BF16)
            st = s_st[pidx]
            rhs = jnp.concatenate([xpb, st.astype(BF16)], axis=0)
            ys = []
            for h in (2 * pidx, 2 * pidx + 1):
                bc = jnp.broadcast_to(cum2[:, h:h + 1], (c, c))
                seg = jnp.exp2(jnp.where(causal, bc - src2_t[h:h + 1, :], -jnp.inf))
                lhs = jnp.concatenate([(cb * seg).astype(BF16), (cg * jnp.exp2(bc)).astype(BF16)],
                                      axis=1)
                ys.append(jnp.dot(lhs, rhs, preferred_element_type=F32))
            h0 = 2 * pidx
            y_scr[:, ps] = jnp.where(lo, ys[0], ys[1]) + d_ref[:, ps] * xp
            el = jnp.where(lo, e_last[:, h0:h0 + 1], e_last[:, h0 + 1:h0 + 2])
            xdec = xp * jnp.where(lo, dec[:, h0:h0 + 1], dec[:, h0 + 1:h0 + 2])
            s_st[pidx] = el * st + jnp.dot(bg_tb, xdec.astype(BF16), preferred_element_type=F32)

    for g in range(SSD_GROUPS):
        gs = slice(g * gw, (g + 1) * gw)
        yg = y_scr[:, gs] * _silu_of_half(z_ref[:, gs])
        yg = yg * lax.rsqrt(jnp.mean(yg * yg, axis=1, keepdims=True) + NORM_EPS)
        y_ref[:, gs] = (yg * gn_ref[:, gs]).astype(y_ref.dtype)


def _ssd_call(p, pdt, wts, state0, bsz, n_chunks, pad_rows, emit_state):
    c = CHUNK
    inner = wts["d"].shape[1]
    gn = wts["cwb"].shape[1]
    n_pairs = inner // LANES
    n_sub, steps, row = _step_layout(n_chunks)
    cs = c * n_sub
    shapes = [(n_pairs, SSD_STATE, LANES), (SUBLANES, inner), (SUBLANES, gn), (SUBLANES, gn)]
    st_in, st_out, st_scr = _state_specs(shapes, emit_state)
    names = ["cwx", "cbx", "cwb", "cbb", "cwc", "cbc", "dtb", "a", "d", "gn"]
    bcol = 2 * inner // gn
    work = [pltpu.VMEM((n_sub, c, inner), F32), pltpu.VMEM((n_sub, c, gn), F32),
            pltpu.VMEM((n_sub, c, gn), F32), pltpu.VMEM((n_sub, c, inner), F32)]
    return pl.pallas_call(
        functools.partial(_ssd_kernel, pad_rows=pad_rows, emit_state=emit_state, n_sub=n_sub),
        grid=(bsz, steps),
        in_specs=[pl.BlockSpec((cs, inner), lambda b, i: (row(b, i), 0)),
                  pl.BlockSpec((cs, inner), lambda b, i: (row(b, i), 1)),
                  pl.BlockSpec((cs, gn), lambda b, i: (row(b, i), bcol)),
                  pl.BlockSpec((cs, gn), lambda b, i: (row(b, i), bcol + 1)),
                  pl.BlockSpec((cs, LANES), lambda b, i: (row(b, i), 0))]
        + [_const_spec(wts[n].shape) for n in names] + st_in,
        out_specs=[pl.BlockSpec((cs, inner), lambda b, i: (row(b, i), 0))] + st_out,
        out_shape=[jax.ShapeDtypeStruct((bsz * n_chunks * c, inner), BF16)]
        + ([jax.ShapeDtypeStruct((bsz,) + s, F32) for s in shapes] if emit_state else []),
        scratch_shapes=work + st_scr,
        compiler_params=_params(2),
        name="ssd_mixer",
    )(p, p, p, p, pdt, *[wts[n] for n in names], *state0)


def _ssd_weights(conv_w, conv_b, dt_bias, a_log, d, gnorm):
    n_heads = dt_bias.shape[0]
    inner = gnorm.shape[0]
    gn = (conv_w.shape[1] - inner) // 2
    pad = lambda v: jnp.zeros((1, LANES), F32).at[0, :n_heads].set(v)
    cb = 0.5 * conv_b.reshape(1, -1)
    conv_w = 0.5 * conv_w
    return dict(cwx=conv_w[:, :inner], cbx=cb[:, :inner],
                cwb=conv_w[:, inner:inner + gn], cbb=cb[:, inner:inner + gn],
                cwc=conv_w[:, inner + gn:], cbc=cb[:, inner + gn:],
                dtb=pad(dt_bias), a=pad(-jnp.exp(a_log)),
                d=jnp.repeat(d, inner // n_heads).reshape(1, inner), gn=gnorm.reshape(1, inner))


def kernel(x, meta_tokens, ab_norm, ab_w_in, s5_lambda_re, s5_lambda_im, s5_log_dt, s5_b_re, s5_b_im, s5_c_re, s5_c_im, s5_d, s5_glu_w, s5_glu_b, ml_conv_w, ml_conv_b, ml_wq, ml_wk, ml_wv, ml_w_gate, ml_b_gate, ml_norm, ml_skip, ab_w_out, ssd_norm, ssd_w_in, ssd_conv_w, ssd_conv_b, ssd_dt_bias, ssd_a_log, ssd_d, ssd_gnorm, ssd_w_out, final_norm):
    bsz, seq, d_model = x.shape
    assert seq % CHUNK == 0 and N_META <= CHUNK
    n_chunks = seq // CHUNK
    s5_w = s5_d.shape[1]
    ml_w = ml_conv_w.shape[2]
    ssd_inner = ssd_gnorm.shape[1]
    ssd_heads = ssd_dt_bias.shape[1]
    ssd_main = ssd_w_in.shape[2] - ssd_heads
    pad_rows = CHUNK - N_META

    half = lambda n: jnp.full((n,), 0.5, F32)
    one = lambda n: jnp.ones((n,), F32)
    w_in0 = (ab_w_in[0] * jnp.concatenate([one(s5_w), half(s5_w), one(ml_w), half(ml_w)])).astype(BF16)
    slab_w = s5_w
    w_out0 = ab_w_out[0].astype(BF16)
    s5_wts = _s5_weights(s5_lambda_re[0], s5_lambda_im[0], s5_log_dt[0], s5_b_re[0], s5_b_im[0],
                         s5_c_re[0], s5_c_im[0], s5_d[0], s5_glu_w[0], s5_glu_b[0])
    ml_wts = _mlstm_weights(ml_conv_w[0], ml_conv_b[0], ml_wq[0], ml_wk[0], ml_wv[0],
                            ml_w_gate[0], ml_b_gate[0], ml_norm[0], ml_skip[0])
    w_in1_t = jnp.swapaxes(ssd_w_in[0], 0, 1)
    w_in1_main = (w_in1_t[:ssd_main]
                  * jnp.concatenate([half(ssd_inner), one(ssd_main - ssd_inner)])[:, None]).astype(BF16)
    w_in1_dt = jnp.zeros((LANES, d_model), F32).at[:ssd_heads].set(w_in1_t[ssd_main:]).astype(BF16)
    w_out1 = ssd_w_out[0].astype(BF16)
    ssd_wts = _ssd_weights(ssd_conv_w[0], ssd_conv_b[0], ssd_dt_bias[0], ssd_a_log[0], ssd_d[0],
                           ssd_gnorm[0])

    n_tiles = s5_wts["apr"].shape[0]
    dh = ml_w // MLSTM_HEADS
    zeros = lambda *s: jnp.zeros(s, F32)
    s5_zero = (zeros(1, n_tiles, 1, LANES), zeros(1, n_tiles, 1, LANES))
    ml_zero = (zeros(1, MLSTM_HEADS, dh, dh + LANES), zeros(1, 1, LANES), zeros(1, SUBLANES, ml_w))
    gn = (ssd_main - 2 * ssd_inner) // 2
    ssd_zero = (zeros(1, ssd_inner // LANES, SSD_STATE, LANES), zeros(1, SUBLANES, ssd_inner),
                zeros(1, SUBLANES, gn), zeros(1, SUBLANES, gn))

    def layer0(h, b, nc, pad, s5_state, ml_state, emit_state):
        p = _norm_proj(h, ab_norm[0], w_in0, name="ab_in_proj")[0]
        y, *ml_out = _mlstm_call(p, 2 * s5_w // slab_w, (2 * s5_w + ml_w) // slab_w, slab_w,
                                 s5_w + ml_w, ml_wts, ml_state, b, nc, pad, emit_state)
        y, sr, si = _s5_call(p, 0, 1, y, 0, s5_wts, s5_state, b, nc, pad)
        h = _res_proj(y, w_out0, h, name="ab_out_proj")
        return h, (sr, si), tuple(ml_out)

    def layer1_mixer(h, b, nc, pad, ssd_state, emit_state):
        p, pdt = _norm_proj(h, ssd_norm[0], w_in1_main, w_in1_dt, w_transposed=True,
                            name="ssd_in_proj")
        y, *st = _ssd_call(p, pdt, ssd_wts, ssd_state, b, nc, pad, emit_state)
        return y, tuple(st)

    hm = jnp.concatenate([zeros(pad_rows, d_model), meta_tokens.astype(F32)], axis=0)
    hm, s5_state, ml_state = layer0(hm, 1, 1, pad_rows, s5_zero, ml_zero, True)
    _, ssd_state = layer1_mixer(hm, 1, 1, pad_rows, ssd_zero, True)

    h = x.reshape(bsz * seq, d_model)
    h, _, _ = layer0(h, bsz, n_chunks, 0, s5_state, ml_state, False)
    y, _ = layer1_mixer(h, bsz, n_chunks, 0, ssd_state, False)
    out = _res_proj_norm(y, w_out1, h, final_norm, name="ssd_out_proj_norm")
    return out.reshape(bsz, seq, d_model)
```
